```python
import jax
import jax.numpy as jnp
from jax import lax
import numpy as np

D_MODEL = 1024
BATCH = 2
SEQ = 16384
DEPTH = 4

CHUNK = 64
HGRN_BLOCK = 16
SB_BLOCK = 128
N_EVEN = (DEPTH + 1) // 2
N_ODD = DEPTH // 2
A_HEADS = 4
A_DK = 128
A_DV = 128
A_QK = A_HEADS * A_DK
A_V = A_HEADS * A_DV
B_HEADS = 4
B_HD = 128
B_WIDTH = B_HEADS * B_HD
P_EVEN = 2 * A_QK + 2 * A_V + 3 * B_WIDTH
MIX_EVEN = A_V + B_WIDTH
C_HEADS = 8
C_DQK = 64
C_DV = 128
C_QK = C_HEADS * C_DQK
C_V = C_HEADS * C_DV
C_CONV = 4
P_ODD = 2 * C_QK + 2 * C_V + 2 * C_HEADS
FFN_DIM = 2816
FFN_CONV = 3
EPS = 1e-6

kernel_name = 'hybrid_hgrn2_stickbreak_mlstm_convglu'


def rmsnorm(x, g):
    xf = x.astype(jnp.float32)
    y = xf * lax.rsqrt(jnp.mean(xf * xf, axis=-1, keepdims=True) + EPS)
    return y * g.astype(jnp.float32)


def modulate(x, g, shift, scale):
    return (rmsnorm(x, g) * (1.0 + scale) + shift).astype(x.dtype)


def head_rmsnorm(o, g):
    b, t, h, d = o.shape
    y = o * lax.rsqrt(jnp.mean(o * o, axis=-1, keepdims=True) + EPS)
    return y.reshape(b, t, h * d) * g.astype(jnp.float32)


def causal_dwconv(x, w, b):
    k = w.shape[0]
    y = lax.conv_general_dilated(x, w[:, None, :].astype(x.dtype), window_strides=(1,),
                                 padding=[(k - 1, 0)], dimension_numbers=('NWC', 'WIO', 'NWC'),
                                 feature_group_count=x.shape[-1])
    return y + b.astype(x.dtype)


def to_chunks(a, size):
    bsz, t = a.shape[0], a.shape[1]
    a = a.reshape((bsz, t // size, size) + a.shape[2:])
    return jnp.swapaxes(jnp.moveaxis(a, 1, 0), 2, 3)


def from_chunks(o):
    n, bsz, h, l, d = o.shape
    return jnp.moveaxis(jnp.swapaxes(o, 2, 3), 0, 1).reshape(bsz, n * l, h, d)


def hgrn2_scan(q, log_f, k, v):
    bsz = q.shape[0]
    mask = jnp.tril(jnp.ones((HGRN_BLOCK, HGRN_BLOCK), dtype=bool))

    def step(state, inp):
        qc, lf, kc, vc = inp
        b = jnp.cumsum(lf, axis=2)
        rel = jnp.where(mask[:, :, None], b[:, :, :, None, :] - b[:, :, None, :, :], -jnp.inf)
        scores = jnp.einsum('bhtd,bhsd,bhtsd->bhts', qc, kc, jnp.exp(rel))
        o = (jnp.einsum('bhts,bhse->bhte', scores, vc)
             + jnp.einsum('bhtd,bhde->bhte', qc * jnp.exp(b), state))
        b_end = b[:, :, -1:, :]
        state = (jnp.exp(b_end[:, :, 0, :])[..., None] * state
                 + jnp.einsum('bhsd,bhse->bhde', kc * jnp.exp(b_end - b), vc))
        return state, o

    s0 = jnp.zeros((bsz, A_HEADS, A_DK, A_DV), jnp.float32)
    xs = (to_chunks(q, HGRN_BLOCK), to_chunks(log_f, HGRN_BLOCK),
          to_chunks(k, HGRN_BLOCK), to_chunks(v, HGRN_BLOCK))
    _, o = lax.scan(step, s0, xs)
    return from_chunks(o)


def stick_breaking(q, k, v):
    bsz, t, h, hd = q.shape
    nb = t // SB_BLOCK

    def blocks(a):
        return a.reshape(bsz, nb, SB_BLOCK, h, hd).transpose(0, 3, 1, 2, 4)

    qb = blocks(q * (hd ** -0.5))
    kb = blocks(k)
    vb = blocks(v)
    incl = jnp.tril(jnp.ones((SB_BLOCK, SB_BLOCK), jnp.float32))
    strict = jnp.tril(jnp.ones((SB_BLOCK, SB_BLOCK), dtype=bool), k=-1)
    out = jnp.zeros((bsz, h, nb, SB_BLOCK, hd), jnp.float32)
    acc = jnp.zeros((bsz, h, nb, SB_BLOCK), jnp.float32)
    for d in range(nb):
        z = jnp.einsum('bhnqd,bhnkd->bhnqk', qb[:, :, d:], kb[:, :, :nb - d]).astype(jnp.float32)
        lk = jax.nn.log_sigmoid(-z)
        if d == 0:
            lk = jnp.where(strict, lk, 0.0)
        r = jnp.einsum('bhnqj,js->bhnqs', lk, incl) + acc[:, :, d:, :, None]
        a = jnp.exp(z + r)
        if d == 0:
            a = jnp.where(strict, a, 0.0)
        out = out.at[:, :, d:].add(
            jnp.einsum('bhnqk,bhnkd->bhnqd', a, vb[:, :, :nb - d].astype(jnp.float32)))
        acc = acc.at[:, :, d:].set(r[..., 0])
    return out.transpose(0, 2, 3, 1, 4).reshape(bsz, t, h, hd)


def mlstm_scan(q, k, v, ig, lf):
    bsz = q.shape[0]
    mask = jnp.tril(jnp.ones((CHUNK, CHUNK), dtype=bool))

    def step(carry, inp):
        cmem, nvec, m = carry
        qc, kc, vc, igc, lfc = inp
        b = jnp.cumsum(lfc, axis=-1)
        log_d = jnp.where(mask, b[..., :, None] - b[..., None, :] + igc[..., None, :], -jnp.inf)
        log_inter = b + m[..., None]
        m_t = jnp.maximum(log_inter, jnp.max(log_d, axis=-1))
        w = jnp.einsum('bhtd,bhsd->bhts', qc, kc) * jnp.exp(log_d - m_t[..., None])
        w_inter = jnp.exp(log_inter - m_t)
        num = (jnp.einsum('bhts,bhse->bhte', w, vc)
               + w_inter[..., None] * jnp.einsum('bhtd,bhde->bhte', qc, cmem))
        den = jnp.sum(w, axis=-1) + w_inter * jnp.einsum('bhtd,bhd->bht', qc, nvec)
        hc = num / jnp.maximum(jnp.abs(den), jnp.exp(-m_t))[..., None]
        m_new = m_t[..., -1]
        w_end = jnp.exp(b[..., -1:] - b + igc - m_new[..., None])
        decay = jnp.exp(b[..., -1] + m - m_new)
        cmem = decay[..., None, None] * cmem + jnp.einsum('bhs,bhsd,bhse->bhde', w_end, kc, vc)
        nvec = decay[..., None] * nvec + jnp.einsum('bhs,bhsd->bhd', w_end, kc)
        return (cmem, nvec, m_new), hc

    init = (jnp.zeros((bsz, C_HEADS, C_DQK, C_DV), jnp.float32),
            jnp.zeros((bsz, C_HEADS, C_DQK), jnp.float32),
            jnp.zeros((bsz, C_HEADS), jnp.float32))
    xs = (to_chunks(q, CHUNK), to_chunks(k, CHUNK), to_chunks(v, CHUNK),
          to_chunks(ig, CHUNK), to_chunks(lf, CHUNK))
    _, hc = lax.scan(step, init, xs)
    return from_chunks(hc)


def even_mixer(h, w_in, lb, norm_g, w_out):
    bsz, t, _ = h.shape
    p = jnp.matmul(h, w_in).astype(jnp.float32)
    splits = [A_QK, 2 * A_QK, 2 * A_QK + A_V, 2 * A_QK + 2 * A_V,
              2 * A_QK + 2 * A_V + B_WIDTH, 2 * A_QK + 2 * A_V + 2 * B_WIDTH]
    qa, fa, ia, ga, qb, kb, vb = jnp.split(p, splits, axis=-1)
    lb_h = lb.astype(jnp.float32).reshape(A_HEADS, A_DK)
    zf = fa.reshape(bsz, t, A_HEADS, A_DK)
    log_f = jnp.logaddexp(jnp.log(lb_h), jnp.log1p(-lb_h) + jax.nn.log_sigmoid(zf))
    k_a = (1.0 - lb_h) * jax.nn.sigmoid(-zf)
    o_a = hgrn2_scan(qa.reshape(bsz, t, A_HEADS, A_DK), log_f, k_a, ia.reshape(bsz, t, A_HEADS, A_DV))
    o_a = head_rmsnorm(o_a, norm_g) * jax.nn.silu(ga)
    o_b = stick_breaking(qb.reshape(bsz, t, B_HEADS, B_HD), kb.reshape(bsz, t, B_HEADS, B_HD),
                         vb.reshape(bsz, t, B_HEADS, B_HD)).reshape(bsz, t, B_WIDTH)
    o = jnp.concatenate([o_a, o_b], axis=-1).astype(h.dtype)
    return jnp.matmul(o, w_out)


def odd_mixer(h, w_in, conv_w, conv_b, gate_b, norm_g, w_out):
    bsz, t, _ = h.shape
    p = jnp.matmul(h, w_in).astype(jnp.float32)
    qk, v, og, gates = jnp.split(p, [2 * C_QK, 2 * C_QK + C_V, 2 * C_QK + 2 * C_V], axis=-1)
    qk = jax.nn.silu(causal_dwconv(qk, conv_w.astype(jnp.float32), conv_b.astype(jnp.float32)))
    q, k = jnp.split(qk, 2, axis=-1)
    q = q.reshape(bsz, t, C_HEADS, C_DQK)
    k = k.reshape(bsz, t, C_HEADS, C_DQK) * (C_DQK ** -0.5)
    gates = gates + gate_b.astype(jnp.float32)
    ig = gates[..., :C_HEADS]
    lf = jax.nn.log_sigmoid(gates[..., C_HEADS:])
    hc = mlstm_scan(q, k, v.reshape(bsz, t, C_HEADS, C_DV), ig, lf)
    o = head_rmsnorm(hc, norm_g) * jax.nn.sigmoid(og)
    return jnp.matmul(o.astype(h.dtype), w_out)


def conv_glu_ffn(h, w_in, conv_w, conv_b, w_out):
    u = jnp.matmul(h, w_in)
    a, g = jnp.split(u, 2, axis=-1)
    a = causal_dwconv(a, conv_w, conv_b)
    return jnp.matmul(jax.nn.gelu(a) * g, w_out)


def setup_inputs(seed: int = 0) -> dict:
    key = jax.random.key(seed)
    ks = jax.random.split(key, 24)

    def nrm(k, shape, scale):
        return jax.random.normal(k, shape, jnp.float32) * scale

    d = D_MODEL
    gate_b = jnp.concatenate([
        nrm(ks[13], (N_ODD, C_HEADS), 0.1),
        jnp.linspace(3.0, 6.0, C_HEADS, dtype=jnp.float32)[None, :] + nrm(ks[14], (N_ODD, C_HEADS), 0.1)], axis=-1)
    return {
        'x': nrm(ks[0], (BATCH, SEQ, d), 1.0),
        'c': nrm(ks[1], (BATCH, d), 1.0),
        'ada_w': nrm(ks[2], (DEPTH, d, 6 * d), 0.5 * d ** -0.5),
        'ada_b': nrm(ks[3], (DEPTH, 6 * d), 0.02),
        'norm_mix_g': 1.0 + nrm(ks[4], (DEPTH, d), 0.02),
        'norm_ffn_g': 1.0 + nrm(ks[5], (DEPTH, d), 0.02),
        'even_w_in': nrm(ks[6], (N_EVEN, d, P_EVEN), d ** -0.5),
        'even_w_out': nrm(ks[7], (N_EVEN, MIX_EVEN, d), MIX_EVEN ** -0.5),
        'hgrn_lb': nrm(ks[8], (N_EVEN, A_QK), 0.5),
        'hgrn_norm_g': 1.0 + nrm(ks[9], (N_EVEN, A_V), 0.02),
        'odd_w_in': nrm(ks[10], (N_ODD, d, P_ODD), d ** -0.5),
        'odd_conv_w': nrm(ks[11], (N_ODD, C_CONV, 2 * C_QK), C_CONV ** -0.5),
        'odd_conv_b': nrm(ks[12], (N_ODD, 2 * C_QK), 0.02),
        'odd_gate_b': gate_b,
        'odd_norm_g': 1.0 + nrm(ks[15], (N_ODD, C_V), 0.02),
        'odd_w_out': nrm(ks[16], (N_ODD, C_V, d), C_V ** -0.5),
        'ffn_w_in': nrm(ks[17], (DEPTH, d, 2 * FFN_DIM), d ** -0.5),
        'ffn_conv_w': nrm(ks[18], (DEPTH, FFN_CONV, FFN_DIM), FFN_CONV ** -0.5),
        'ffn_conv_b': nrm(ks[19], (DEPTH, FFN_DIM), 0.02),
        'ffn_w_out': nrm(ks[20], (DEPTH, FFN_DIM, d), FFN_DIM ** -0.5),
        'final_g': 1.0 + nrm(ks[21], (d,), 0.02),
    }


def reference(x, c, ada_w, ada_b, norm_mix_g, norm_ffn_g, even_w_in, even_w_out, hgrn_lb,
              hgrn_norm_g, odd_w_in, odd_conv_w, odd_conv_b, odd_gate_b, odd_norm_g, odd_w_out,
              ffn_w_in, ffn_conv_w, ffn_conv_b, ffn_w_out, final_g):
    lb_all = jnp.cumsum(jax.nn.softmax(hgrn_lb.astype(jnp.float32), axis=0), axis=0)
    lb_all = lb_all - lb_all[:1]
    cs = jax.nn.silu(c)
    for layer in range(DEPTH):
        j = layer // 2
        mod = jnp.matmul(cs, ada_w[layer]) + ada_b[layer]
        sh_m, sc_m, g_m, sh_f, sc_f, g_f = [m[:, None, :] for m in jnp.split(mod, 6, axis=-1)]
        h = modulate(x, norm_mix_g[layer], sh_m, sc_m)
        if layer % 2 == 0:
            y = even_mixer(h, even_w_in[j], lb_all[j], hgrn_norm_g[j], even_w_out[j])
        else:
            y = odd_mixer(h, odd_w_in[j], odd_conv_w[j], odd_conv_b[j], odd_gate_b[j],
                          odd_norm_g[j], odd_w_out[j])
        x = x + (g_m * y).astype(x.dtype)
        h = modulate(x, norm_ffn_g[layer], sh_f, sc_f)
        x = x + (g_f * conv_glu_ffn(h, ffn_w_in[layer], ffn_conv_w[layer], ffn_conv_b[layer],
                                    ffn_w_out[layer])).astype(x.dtype)
    return rmsnorm(x, final_g).astype(x.dtype)
```

```python
import functools
import math

import jax
import jax.numpy as jnp
from jax import lax
from jax.experimental import pallas as pl
from jax.experimental.pallas import tpu as pltpu

F32 = jnp.float32
BF16 = jnp.bfloat16

EPS = 1e-6
LOG2E = 1.4426950408889634

A_HEADS, A_DK, A_DV = 4, 128, 128
B_HEADS, B_HD = 4, 128
C_HEADS, C_DQK, C_DV = 8, 64, 128
C_CONV = 4
FFN_CONV = 3
HGRN_BLOCK = 16
MLSTM_CHUNK = 64

VMEM_LIMIT_BYTES = 56 * 1024 * 1024


def _cparams(*sem):
    return pltpu.CompilerParams(dimension_semantics=sem, vmem_limit_bytes=VMEM_LIMIT_BYTES)


def _dot(a, b):
    return jnp.dot(a, b, preferred_element_type=F32)


def _dot_nt(a, b):
    return lax.dot_general(a, b, (((1,), (1,)), ((), ())), preferred_element_type=F32)


def _dot_tn(a, b):
    return lax.dot_general(a, b, (((0,), (0,)), ((), ())), preferred_element_type=F32)


def _dot_f32(a, b):
    return jnp.dot(a, b, preferred_element_type=F32, precision=lax.Precision.HIGHEST)


def _log_sigmoid(z):
    return jnp.minimum(z, 0.0) - jnp.log1p(jnp.exp(-jnp.abs(z)))


def _modulated_norm(x, g, sh, sc):
    ms = jnp.mean(x * x, axis=-1, keepdims=True)
    return (x * lax.rsqrt(ms + EPS)) * g * (1.0 + sc) + sh


def _ada_kernel(c_ref, w_ref, b_ref, o_ref):
    c = c_ref[...]
    cs = c * jax.nn.sigmoid(c)
    o_ref[0] = _dot_f32(cs, w_ref[0]) + b_ref[0]


def _ada(c_pad, ada_w, ada_b):
    depth, d, n = ada_w.shape
    tn = 1024
    return pl.pallas_call(
        _ada_kernel,
        grid=(depth, n // tn),
        in_specs=[
            pl.BlockSpec((c_pad.shape[0], d), lambda l, j: (0, 0)),
            pl.BlockSpec((1, d, tn), lambda l, j: (l, 0, j)),
            pl.BlockSpec((1, 1, tn), lambda l, j: (l, 0, j)),
        ],
        out_specs=pl.BlockSpec((1, c_pad.shape[0], tn), lambda l, j: (l, 0, j)),
        out_shape=jax.ShapeDtypeStruct((depth, c_pad.shape[0], n), F32),
        compiler_params=_cparams("arbitrary", "arbitrary"),
        name="ada_mod",
    )(c_pad, ada_w, ada_b.reshape(depth, 1, n))


def _even_inproj_kernel(x_ref, g_ref, sh_ref, sc_ref, w_ref, pa_ref, q_ref, k_ref, v_ref, *, na, nb):
    h = _modulated_norm(x_ref[...], g_ref[...], sh_ref[0], sc_ref[0]).astype(BF16)
    cw = 512
    for c in range(na // cw):
        pa_ref[:, c * cw:(c + 1) * cw] = _dot(h, w_ref[:, c * cw:(c + 1) * cw])
    q_ref[...] = (_dot(h, w_ref[:, na:na + nb]) * (B_HD ** -0.5 * LOG2E)).astype(BF16)
    k_ref[...] = _dot(h, w_ref[:, na + nb:na + 2 * nb]).astype(BF16)
    v_ref[...] = _dot(h, w_ref[:, na + 2 * nb:na + 3 * nb]).astype(BF16)


def _even_inproj(x, g, sh, sc, w_bf16, seq):
    n, d = x.shape
    na = 2 * A_HEADS * A_DK + 2 * A_HEADS * A_DV
    nb = B_HEADS * B_HD
    tm = 512
    tpb = seq // tm
    row = lambda i: (i, 0)
    per_batch = lambda i: (i // tpb, 0, 0)
    return pl.pallas_call(
        functools.partial(_even_inproj_kernel, na=na, nb=nb),
        grid=(n // tm,),
        in_specs=[
            pl.BlockSpec((tm, d), row),
            pl.BlockSpec((1, d), lambda i: (0, 0)),
            pl.BlockSpec((1, 1, d), per_batch),
            pl.BlockSpec((1, 1, d), per_batch),
            pl.BlockSpec(w_bf16.shape, lambda i: (0, 0)),
        ],
        out_specs=[
            pl.BlockSpec((tm, na), row),
            pl.BlockSpec((tm, nb), row),
            pl.BlockSpec((tm, nb), row),
            pl.BlockSpec((tm, nb), row),
        ],
        out_shape=[
            jax.ShapeDtypeStruct((n, na), F32),
            jax.ShapeDtypeStruct((n, nb), BF16),
            jax.ShapeDtypeStruct((n, nb), BF16),
            jax.ShapeDtypeStruct((n, nb), BF16),
        ],
        compiler_params=_cparams("arbitrary"),
        name="even_inproj",
    )(x, g, sh, sc, w_bf16)


HGRN_SUB = 128


def _hgrn_kernel(lbp_ref, ng_ref, q_ref, z_ref, v_ref, ga_ref, o_ref, st_ref, oi_ref, *, layer_j, tblk):
    t_idx = pl.program_id(2)

    @pl.when(t_idx == 0)
    def _():
        st_ref[...] = jnp.zeros_like(st_ref)

    lbp = lbp_ref[...]
    e = jnp.exp(lbp - jnp.max(lbp, axis=0, keepdims=True))
    sm = e / jnp.sum(e, axis=0, keepdims=True)
    lb = jnp.zeros((1, A_DK), F32)
    for r in range(1, layer_j + 1):
        lb = lb + sm[r:r + 1]
    log_lb = jnp.log(lb)
    log_1m_lb = jnp.log1p(-lb)
    ng = ng_ref[...]

    sub = HGRN_SUB
    blk = HGRN_BLOCK
    rows = lax.broadcasted_iota(jnp.int32, (sub, sub), 0)
    cols = lax.broadcasted_iota(jnp.int32, (sub, sub), 1)
    tri = jnp.where((rows // blk == cols // blk) & (cols <= rows), 1.0, 0.0).astype(F32)
    pos = lax.broadcasted_iota(jnp.int32, (sub, A_DK), 0) % blk

    def step(sb, carry):
        r0 = pl.multiple_of(sb * sub, sub)
        q = q_ref[0, pl.ds(r0, sub), :]
        zf = z_ref[0, pl.ds(r0, sub), :]
        v = v_ref[0, pl.ds(r0, sub), :]
        ls = _log_sigmoid(zf)
        l2 = log_1m_lb + ls
        mx = jnp.maximum(log_lb, l2)
        log_f = mx + jnp.log1p(jnp.exp(-jnp.abs(log_lb - l2)))
        k = (1.0 - lb) * jnp.exp(ls - zf)
        bc = _dot_f32(tri, log_f)

        o = jnp.sum(q * k, axis=-1, keepdims=True) * v
        for j in range(1, blk):
            bs = pltpu.roll(bc, j, 0)
            ks = pltpu.roll(k, j, 0)
            vs = pltpu.roll(v, j, 0)
            w = jnp.where(pos >= j, jnp.exp(bc - bs), 0.0)
            o = o + jnp.sum(q * ks * w, axis=-1, keepdims=True) * vs

        for c in range(sub // blk):
            lo = c * blk
            bcc = bc[lo:lo + blk]
            bend = bcc[blk - 1:blk]
            st = st_ref[...]
            qt = (q[lo:lo + blk] * jnp.exp(bcc)).astype(BF16)
            oi_ref[lo:lo + blk, :] = _dot_nt(qt, st.astype(BF16))
            kt = (k[lo:lo + blk] * jnp.exp(bend - bcc)).astype(BF16)
            st_ref[...] = st * jnp.exp(bend) + _dot_tn(v[lo:lo + blk].astype(BF16), kt)

        o = o + oi_ref[...]
        y = o * lax.rsqrt(jnp.mean(o * o, axis=-1, keepdims=True) + EPS) * ng
        ga = ga_ref[0, pl.ds(r0, sub), :]
        o_ref[0, pl.ds(r0, sub), :] = (y * (ga * jax.nn.sigmoid(ga))).astype(o_ref.dtype)
        return carry

    lax.fori_loop(0, tblk // sub, step, 0)


def _hgrn(pa3, hgrn_lb, norm_g, layer_j):
    bsz, seq, _ = pa3.shape
    tblk = 512
    nh = A_HEADS
    spec = lambda off: pl.BlockSpec((1, tblk, A_DK), lambda b, h, t: (b, t, off + h))
    return pl.pallas_call(
        functools.partial(_hgrn_kernel, layer_j=layer_j, tblk=tblk),
        grid=(bsz, nh, seq // tblk),
        in_specs=[
            pl.BlockSpec((hgrn_lb.shape[0], A_DK), lambda b, h, t: (0, h)),
            pl.BlockSpec((1, A_DV), lambda b, h, t: (0, h)),
            spec(0), spec(nh), spec(2 * nh), spec(3 * nh),
        ],
        out_specs=pl.BlockSpec((1, tblk, A_DV), lambda b, h, t: (b, t, h)),
        out_shape=jax.ShapeDtypeStruct((bsz, seq, nh * A_DV), BF16),
        scratch_shapes=[pltpu.VMEM((A_DV, A_DK), F32), pltpu.VMEM((HGRN_SUB, A_DV), F32)],
        compiler_params=_cparams("arbitrary", "arbitrary", "arbitrary"),
        name="hgrn2",
    )(hgrn_lb, norm_g, pa3, pa3, pa3, pa3)


SB_TQ = 512
SB_TK = 256
SB_HPG = 2


def _sb_kernel(q_ref, k_ref, v_ref, o_ref, acc_ref, out_ref, *, tq, tk, hpg):
    i = pl.program_id(2)
    hd = B_HD
    r = lax.broadcasted_iota(jnp.int32, (tk, tk), 0)
    c = lax.broadcasted_iota(jnp.int32, (tk, tk), 1)
    incl = jnp.where(r >= c, 1.0, 0.0).astype(BF16)
    acc_ref[...] = jnp.zeros_like(acc_ref)
    out_ref[...] = jnp.zeros_like(out_ref)

    def tile(j, masked):
        k0 = pl.multiple_of(j * tk, tk)
        if masked:
            qpos = i * tq + lax.broadcasted_iota(jnp.int32, (tq, tk), 0)
            kpos = j * tk + lax.broadcasted_iota(jnp.int32, (tq, tk), 1)
            strict = kpos < qpos
        for h in range(hpg):
            hs = slice(h * hd, (h + 1) * hd)
            z = _dot_nt(q_ref[:, hs], k_ref[pl.ds(k0, tk), hs])
            sp = jnp.maximum(z, 0.0) + jnp.log2(1.0 + jnp.exp2(-jnp.abs(z)))
            if masked:
                sp = jnp.where(strict, sp, 0.0)
            rs = _dot(sp.astype(BF16), incl)
            acc = acc_ref[h]
            tot = rs + jnp.concatenate([acc] * (tk // hd), axis=1)
            a = jnp.exp2(z - tot)
            if masked:
                a = jnp.where(strict, a, 0.0)
            out_ref[h] += _dot(a.astype(BF16), v_ref[pl.ds(k0, tk), hs])
            acc_ref[h] = jnp.broadcast_to(tot[:, 0:1], (tq, hd))

    ndiag = tq // tk
    for d in range(ndiag):
        tile((i + 1) * ndiag - 1 - d, True)

    def body(s, carry):
        tile(i * ndiag - 1 - s, False)
        return carry

    lax.fori_loop(0, i * ndiag, body, 0)
    for h in range(hpg):
        o_ref[:, h * hd:(h + 1) * hd] = out_ref[h].astype(o_ref.dtype)


def _stick_breaking(q, k, v, bsz, seq):
    n, width = q.shape
    tq, tk, hpg = min(SB_TQ, seq), min(SB_TK, seq), SB_HPG
    gw = hpg * B_HD
    nq = seq // tq
    return pl.pallas_call(
        functools.partial(_sb_kernel, tq=tq, tk=tk, hpg=hpg),
        grid=(bsz, width // gw, nq),
        in_specs=[
            pl.BlockSpec((tq, gw), lambda b, g, i: (b * nq + i, g)),
            pl.BlockSpec((seq, gw), lambda b, g, i: (b, g)),
            pl.BlockSpec((seq, gw), lambda b, g, i: (b, g)),
        ],
        out_specs=pl.BlockSpec((tq, gw), lambda b, g, i: (b * nq + i, g)),
        out_shape=jax.ShapeDtypeStruct((n, width), BF16),
        scratch_shapes=[pltpu.VMEM((hpg, tq, B_HD), F32), pltpu.VMEM((hpg, tq, B_HD), F32)],
        compiler_params=_cparams("arbitrary", "arbitrary", "arbitrary"),
        name="stick_breaking",
    )(q, k, v)


def _outproj_kernel(*refs, nparts):
    x_ref, gate_ref = refs[0], refs[1]
    parts = refs[2:2 + nparts]
    ws = refs[2 + nparts:2 + 2 * nparts]
    o_ref = refs[2 + 2 * nparts]
    y = _dot(parts[0][...], ws[0][...])
    for p, w in zip(parts[1:], ws[1:]):
        y = y + _dot(p[...], w[...])
    o_ref[...] = x_ref[...] + gate_ref[0] * y


def _outproj(x, gate, parts, ws, seq):
    n, d = x.shape
    tm = 1024
    tpb = seq // tm
    row = lambda i: (i, 0)
    nparts = len(parts)
    return pl.pallas_call(
        functools.partial(_outproj_kernel, nparts=nparts),
        grid=(n // tm,),
        in_specs=[pl.BlockSpec((tm, d), row), pl.BlockSpec((1, 1, d), lambda i: (i // tpb, 0, 0))]
        + [pl.BlockSpec((tm, p.shape[1]), row) for p in parts]
        + [pl.BlockSpec(w.shape, lambda i: (0, 0)) for w in ws],
        out_specs=pl.BlockSpec((tm, d), row),
        out_shape=jax.ShapeDtypeStruct((n, d), F32),
        compiler_params=_cparams("arbitrary"),
        name="mixer_outproj",
    )(x, gate, *parts, *ws)


FFN_CHUNK = 256
HALO = 8


def _gelu_tanh(a):
    c = math.sqrt(2.0 / math.pi)
    return 0.5 * a * (1.0 + jnp.tanh(c * (a + 0.044715 * (a * a * a))))


def _ffn_kernel(x_ref, g_ref, sh_ref, sc_ref, gate_ref, win_ref, cw_ref, cb_ref, wout_ref, o_ref,
                halo_ref, abuf_ref, act_ref, *, f, tpb, tm):
    i = pl.program_id(0)

    @pl.when(i % tpb == 0)
    def _():
        halo_ref[...] = jnp.zeros_like(halo_ref)

    x = x_ref[...]
    h = _modulated_norm(x, g_ref[...], sh_ref[0], sc_ref[0]).astype(BF16)
    fc = FFN_CHUNK
    for c in range(f // fc):
        cs = slice(c * fc, (c + 1) * fc)
        a = _dot(h, win_ref[:, cs])
        gg = _dot(h, win_ref[:, f + c * fc:f + (c + 1) * fc])
        abuf_ref[0:HALO, :] = halo_ref[:, cs]
        abuf_ref[HALO:HALO + tm, :] = a
        halo_ref[:, cs] = a[tm - HALO:tm]
        cw = cw_ref[:, cs]
        conv = a * cw[FFN_CONV - 1:FFN_CONV] + cb_ref[:, cs]
        for tap in range(1, FFN_CONV):
            conv = conv + abuf_ref[HALO - tap:HALO - tap + tm, :] * cw[FFN_CONV - 1 - tap:FFN_CONV - tap]
        act_ref[:, cs] = (_gelu_tanh(conv) * gg).astype(BF16)
    y = _dot(act_ref[...], wout_ref[...])
    o_ref[...] = x + gate_ref[0] * y


def _ffn(x, g, sh, sc, gate, win, cw, cb, wout, seq):
    n, d = x.shape
    f = wout.shape[0]
    tm = 512
    tpb = seq // tm
    row = lambda i: (i, 0)
    per_batch = lambda i: (i // tpb, 0, 0)
    const = lambda i: (0, 0)
    return pl.pallas_call(
        functools.partial(_ffn_kernel, f=f, tpb=tpb, tm=tm),
        grid=(n // tm,),
        in_specs=[
            pl.BlockSpec((tm, d), row),
            pl.BlockSpec((1, d), const),
            pl.BlockSpec((1, 1, d), per_batch),
            pl.BlockSpec((1, 1, d), per_batch),
            pl.BlockSpec((1, 1, d), per_batch),
            pl.BlockSpec(win.shape, const, pipeline_mode=pl.Buffered(1)),
            pl.BlockSpec(cw.shape, const),
            pl.BlockSpec(cb.shape, const),
            pl.BlockSpec(wout.shape, const, pipeline_mode=pl.Buffered(1)),
        ],
        out_specs=pl.BlockSpec((tm, d), row),
        out_shape=jax.ShapeDtypeStruct((n, d), F32),
        scratch_shapes=[
            pltpu.VMEM((HALO, f), F32),
            pltpu.VMEM((HALO + tm, FFN_CHUNK), F32),
            pltpu.VMEM((tm, f), BF16),
        ],
        compiler_params=_cparams("arbitrary"),
        name="convglu_ffn",
    )(x, g, sh, sc, gate, win, cw, cb, wout)


def _odd_inproj_kernel(x_ref, g_ref, sh_ref, sc_ref, w_ref, wgt_ref, cw_ref, cb_ref, gb_ref, gbt_ref,
                       q_ref, k_ref, v_ref, og_ref, gc_ref, gr_ref, halo_ref, abuf_ref, *, tpb, tm, nqk, nv):
    i = pl.program_id(0)

    @pl.when(i % tpb == 0)
    def _():
        halo_ref[...] = jnp.zeros_like(halo_ref)

    h = _modulated_norm(x_ref[...], g_ref[...], sh_ref[0], sc_ref[0]).astype(BF16)
    half = nqk // 2
    for c, (dst, scale) in enumerate(((q_ref, 1.0), (k_ref, C_DQK ** -0.5))):
        cs = slice(c * half, (c + 1) * half)
        a = _dot(h, w_ref[:, cs])
        abuf_ref[0:HALO, :] = halo_ref[:, cs]
        abuf_ref[HALO:HALO + tm, :] = a
        halo_ref[:, cs] = a[tm - HALO:tm]
        cw = cw_ref[:, cs]
        conv = a * cw[C_CONV - 1:C_CONV] + cb_ref[:, cs]
        for tap in range(1, C_CONV):
            conv = conv + abuf_ref[HALO - tap:HALO - tap + tm, :] * cw[C_CONV - 1 - tap:C_CONV - tap]
        s = conv * jax.nn.sigmoid(conv)
        dst[...] = (s * scale).astype(BF16)
    for c in range(nv // 512):
        cs = slice(c * 512, (c + 1) * 512)
        v_ref[:, cs] = _dot(h, w_ref[:, nqk + c * 512:nqk + (c + 1) * 512]).astype(BF16)
        og_ref[:, cs] = _dot(h, w_ref[:, nqk + nv + c * 512:nqk + nv + (c + 1) * 512])
    nh = C_HEADS
    gcol = _dot(h, w_ref[:, nqk + 2 * nv:nqk + 2 * nv + 2 * nh]) + gb_ref[...]
    grow = _dot_nt(wgt_ref[...], h) + gbt_ref[...]
    is_f_col = lax.broadcasted_iota(jnp.int32, gcol.shape, 1) >= nh
    is_f_row = lax.broadcasted_iota(jnp.int32, grow.shape, 0) >= nh
    gc_ref[...] = jnp.where(is_f_col, _log_sigmoid(gcol), gcol)
    grow = jnp.where(is_f_row, _log_sigmoid(grow), grow)
    L = MLSTM_CHUNK
    for ch in range(tm // L):
        gr_ref[ch] = grow[:, ch * L:(ch + 1) * L]


def _odd_inproj(x, g, sh, sc, w, wgt, cw, cb, gb, gbt, seq):
    n, d = x.shape
    nqk = 2 * C_HEADS * C_DQK
    nv = C_HEADS * C_DV
    nh2 = 2 * C_HEADS
    tm = 512
    tpb = seq // tm
    row = lambda i: (i, 0)
    per_batch = lambda i: (i // tpb, 0, 0)
    const = lambda i: (0, 0)
    return pl.pallas_call(
        functools.partial(_odd_inproj_kernel, tpb=tpb, tm=tm, nqk=nqk, nv=nv),
        grid=(n // tm,),
        in_specs=[
            pl.BlockSpec((tm, d), row),
            pl.BlockSpec((1, d), const),
            pl.BlockSpec((1, 1, d), per_batch),
            pl.BlockSpec((1, 1, d), per_batch),
            pl.BlockSpec(w.shape, const),
            pl.BlockSpec(wgt.shape, const),
            pl.BlockSpec(cw.shape, const),
            pl.BlockSpec(cb.shape, const),
            pl.BlockSpec(gb.shape, const),
            pl.BlockSpec(gbt.shape, const),
        ],
        out_specs=[
            pl.BlockSpec((tm, nqk // 2), row),
            pl.BlockSpec((tm, nqk // 2), row),
            pl.BlockSpec((tm, nv), row),
            pl.BlockSpec((tm, nv), row),
            pl.BlockSpec((tm, nh2), row),
            pl.BlockSpec((tm // MLSTM_CHUNK, nh2, MLSTM_CHUNK), lambda i: (i, 0, 0)),
        ],
        out_shape=[
            jax.ShapeDtypeStruct((n, nqk // 2), BF16),
            jax.ShapeDtypeStruct((n, nqk // 2), BF16),
            jax.ShapeDtypeStruct((n, nv), BF16),
            jax.ShapeDtypeStruct((n, nv), F32),
            jax.ShapeDtypeStruct((n, nh2), F32),
            jax.ShapeDtypeStruct((n // MLSTM_CHUNK, nh2, MLSTM_CHUNK), F32),
        ],
        scratch_shapes=[pltpu.VMEM((HALO, nqk), F32), pltpu.VMEM((HALO + tm, nqk // 2), F32)],
        compiler_params=_cparams("arbitrary"),
        name="odd_inproj",
    )(x, g, sh, sc, w, wgt, cw, cb, gb, gbt)


def _mlstm_kernel(q_ref, k_ref, v_ref, og_ref, gc_ref, gr_ref, ng_ref, o_ref, cm_ref, nv_ref, m_ref, *, tblk):
    p = pl.program_id(1)
    t_idx = pl.program_id(2)
    L = MLSTM_CHUNK
    dq = C_DQK

    @pl.when(t_idx == 0)
    def _():
        cm_ref[...] = jnp.zeros_like(cm_ref)
        nv_ref[...] = jnp.zeros_like(nv_ref)
        m_ref[...] = jnp.zeros_like(m_ref)

    r = lax.broadcasted_iota(jnp.int32, (L, L), 0)
    c = lax.broadcasted_iota(jnp.int32, (L, L), 1)
    causal = c <= r
    tril = jnp.where(causal, 1.0, 0.0).astype(F32)
    triu = jnp.where(r <= c, 1.0, 0.0).astype(F32)
    lane = lax.broadcasted_iota(jnp.int32, (1, 2 * dq), 1)
    sub = lax.broadcasted_iota(jnp.int32, (2 * dq, 1), 0)
    nh = C_HEADS

    def step(ci, carry):
        r0 = pl.multiple_of(ci * L, L)
        qp = q_ref[pl.ds(r0, L), :]
        kp = k_ref[pl.ds(r0, L), :]
        gcol = gc_ref[pl.ds(r0, L), :]
        grow = gr_ref[ci]
        bcol_all = _dot_f32(tril, gcol)
        brow_all = _dot_f32(grow, triu)
        cm_old = cm_ref[...]
        cm_new = jnp.zeros_like(cm_old)
        dec_rows = jnp.zeros((2 * dq, 1), F32)
        for x in range(2):
            hidx = 2 * p + x
            sel_c = (lax.broadcasted_iota(jnp.int32, (1, 2 * nh), 1) == hidx)
            sel_fc = (lax.broadcasted_iota(jnp.int32, (1, 2 * nh), 1) == hidx + nh)
            sel_r = (lax.broadcasted_iota(jnp.int32, (2 * nh, 1), 0) == hidx)
            sel_fr = (lax.broadcasted_iota(jnp.int32, (2 * nh, 1), 0) == hidx + nh)
            ig_col = jnp.sum(jnp.where(sel_c, gcol, 0.0), axis=1, keepdims=True)
            b_col = jnp.sum(jnp.where(sel_fc, bcol_all, 0.0), axis=1, keepdims=True)
            ig_row = jnp.sum(jnp.where(sel_r, grow, 0.0), axis=0, keepdims=True)
            b_row = jnp.sum(jnp.where(sel_fr, brow_all, 0.0), axis=0, keepdims=True)
            hm = (lane >= x * dq) & (lane < (x + 1) * dq)
            qx = jnp.where(hm, qp, jnp.zeros_like(qp))
            kx = jnp.where(hm, kp, jnp.zeros_like(kp))
            vx = v_ref[pl.ds(r0, L), x * C_DV:(x + 1) * C_DV]
            m_prev = m_ref[x:x + 1, 0:1]
            n_prev = nv_ref[x:x + 1, :]

            log_d = jnp.where(causal, b_col - b_row + ig_row, -jnp.inf)
            log_inter = b_col + m_prev
            m_t = jnp.maximum(log_inter, jnp.max(log_d, axis=-1, keepdims=True))
            w = _dot_nt(qx, kp) * jnp.exp(log_d - m_t)
            w_inter = jnp.exp(log_inter - m_t)
            num = _dot(w.astype(BF16), vx) + w_inter * _dot(qx, cm_old.astype(BF16))
            qn = jnp.sum(qx.astype(F32) * n_prev, axis=-1, keepdims=True)
            den = jnp.sum(w, axis=-1, keepdims=True) + w_inter * qn
            hc = num / jnp.maximum(jnp.abs(den), jnp.exp(-m_t))

            m_new = m_t[L - 1:L]
            b_last = b_col[L - 1:L]
            w_end = jnp.exp(b_last - b_col + ig_col - m_new)
            decay = jnp.exp(b_last + m_prev - m_new)
            kw = kx.astype(F32) * w_end
            cm_new = cm_new + _dot_tn(kw.astype(BF16), vx)
            dec_rows = dec_rows + jnp.where((sub >= x * dq) & (sub < (x + 1) * dq), decay, 0.0)
            nv_ref[x:x + 1, :] = decay * n_prev + jnp.sum(kw, axis=0, keepdims=True)
            m_ref[x:x + 1, :] = jnp.broadcast_to(m_new, (1, m_ref.shape[1]))

            y = hc * lax.rsqrt(jnp.mean(hc * hc, axis=-1, keepdims=True) + EPS)
            y = y * ng_ref[:, x * C_DV:(x + 1) * C_DV]
            og = og_ref[pl.ds(r0, L), x * C_DV:(x + 1) * C_DV]
            o_ref[pl.ds(r0, L), x * C_DV:(x + 1) * C_DV] = (y * jax.nn.sigmoid(og)).astype(o_ref.dtype)
        cm_ref[...] = dec_rows * cm_old + cm_new
        return carry

    lax.fori_loop(0, tblk // L, step, 0)


def _mlstm(q, k, v, og, gc, gr, norm_g, bsz, seq):
    n = q.shape[0]
    tblk = 512
    nt = seq // tblk
    npairs = C_HEADS // 2
    pw = 2 * C_DQK
    vw = 2 * C_DV
    tok = lambda b, p, t: (b * nt + t, p)
    return pl.pallas_call(
        functools.partial(_mlstm_kernel, tblk=tblk),
        grid=(bsz, npairs, nt),
        in_specs=[
            pl.BlockSpec((tblk, pw), tok),
            pl.BlockSpec((tblk, pw), tok),
            pl.BlockSpec((tblk, vw), tok),
            pl.BlockSpec((tblk, vw), tok),
            pl.BlockSpec((tblk, 2 * C_HEADS), lambda b, p, t: (b * nt + t, 0)),
            pl.BlockSpec((tblk // MLSTM_CHUNK, 2 * C_HEADS, MLSTM_CHUNK), lambda b, p, t: (b * nt + t, 0, 0)),
            pl.BlockSpec((1, vw), lambda b, p, t: (0, p)),
        ],
        out_specs=pl.BlockSpec((tblk, vw), tok),
        out_shape=jax.ShapeDtypeStruct((n, C_HEADS * C_DV), BF16),
        scratch_shapes=[
            pltpu.VMEM((pw, C_DV), F32),
            pltpu.VMEM((8, pw), F32),
            pltpu.VMEM((8, 128), F32),
        ],
        compiler_params=_cparams("arbitrary", "arbitrary", "arbitrary"),
        name="mlstm",
    )(q, k, v, og, gc, gr, norm_g)


def _final_norm_kernel(x_ref, g_ref, o_ref):
    x = x_ref[...]
    o_ref[...] = x * lax.rsqrt(jnp.mean(x * x, axis=-1, keepdims=True) + EPS) * g_ref[...]


def _final_norm(x, g):
    n, d = x.shape
    tm = 1024
    return pl.pallas_call(
        _final_norm_kernel,
        grid=(n // tm,),
        in_specs=[pl.BlockSpec((tm, d), lambda i: (i, 0)), pl.BlockSpec((1, d), lambda i: (0, 0))],
        out_specs=pl.BlockSpec((tm, d), lambda i: (i, 0)),
        out_shape=jax.ShapeDtypeStruct((n, d), F32),
        compiler_params=_cparams("arbitrary"),
        name="final_norm",
    )(x, g)


def kernel(x, c, ada_w, ada_b, norm_mix_g, norm_ffn_g, even_w_in, even_w_out, hgrn_lb, hgrn_norm_g,
           odd_w_in, odd_conv_w, odd_conv_b, odd_gate_b, odd_norm_g, odd_w_out, ffn_w_in, ffn_conv_w,
           ffn_conv_b, ffn_w_out, final_g):
    bsz, seq, d = x.shape
    depth = ada_w.shape[0]
    n = bsz * seq
    xf = x.reshape(n, d)

    c_pad = jnp.zeros((8, d), F32).at[:bsz].set(c)
    mod = _ada(c_pad, ada_w, ada_b)[:, :bsz]
    mod = mod.reshape(depth, bsz, 6, 1, d)

    nqk = 2 * C_HEADS * C_DQK
    nv = C_HEADS * C_DV
    for layer in range(depth):
        j = layer // 2
        sh_m, sc_m, g_m, sh_f, sc_f, g_f = (mod[layer, :, i] for i in range(6))
        gmix = norm_mix_g[layer].reshape(1, d)
        if layer % 2 == 0:
            pa, qb, kb, vb = _even_inproj(xf, gmix, sh_m, sc_m, even_w_in[j].astype(BF16), seq)
            oa = _hgrn(pa.reshape(bsz, seq, -1), hgrn_lb, hgrn_norm_g[j].reshape(1, -1), j)
            ob = _stick_breaking(qb, kb, vb, bsz, seq)
            wo = even_w_out[j].astype(BF16)
            na = A_HEADS * A_DV
            xf = _outproj(xf, g_m, [oa.reshape(n, -1), ob], [wo[:na], wo[na:]], seq)
        else:
            w = odd_w_in[j]
            wgt = w[:, nqk + 2 * nv:].T.astype(BF16)
            gb = odd_gate_b[j].reshape(1, -1)
            q, k, v, og, gc, gr = _odd_inproj(
                xf, gmix, sh_m, sc_m, w.astype(BF16), wgt, odd_conv_w[j], odd_conv_b[j].reshape(1, -1),
                gb, gb.reshape(-1, 1), seq)
            o = _mlstm(q, k, v, og, gc, gr, odd_norm_g[j].reshape(1, -1), bsz, seq)
            xf = _outproj(xf, g_m, [o], [odd_w_out[j].astype(BF16)], seq)
        xf = _ffn(xf, norm_ffn_g[layer].reshape(1, d), sh_f, sc_f, g_f, ffn_w_in[layer].astype(BF16),
                  ffn_conv_w[layer], ffn_conv_b[layer].reshape(1, -1), ffn_w_out[layer].astype(BF16), seq)
    return _final_norm(xf, final_g.reshape(1, d)).reshape(bsz, seq, d)
```

```python
import functools
import math

import jax
import jax.numpy as jnp
import numpy as np
from jax import lax
from jax.experimental import pallas as pl
from jax.experimental.pallas import tpu as pltpu

F32 = jnp.float32
BF16 = jnp.bfloat16

EPS = 1e-6
LOG2E = 1.4426950408889634

A_HEADS, A_DK, A_DV = 4, 128, 128
B_HEADS, B_HD = 4, 128
C_HEADS, C_DQK, C_DV = 8, 64, 128
C_CONV = 4
FFN_CONV = 3

VMEM_LIMIT_BYTES = 56 * 1024 * 1024


def _cparams(*sem):
    return pltpu.CompilerParams(dimension_semantics=sem, vmem_limit_bytes=VMEM_LIMIT_BYTES)


def _dot(a, b):
    return jnp.dot(a, b, preferred_element_type=F32)


def _dot_nt(a, b):
    return lax.dot_general(a, b, (((1,), (1,)), ((), ())), preferred_element_type=F32)


def _dot_tn(a, b):
    return lax.dot_general(a, b, (((0,), (0,)), ((), ())), preferred_element_type=F32)


def _dot_f32(a, b):
    return jnp.dot(a, b, preferred_element_type=F32, precision=lax.Precision.HIGHEST)


def _log_sigmoid(z):
    return jnp.minimum(z, 0.0) - jnp.log1p(jnp.exp(-jnp.abs(z)))


def _modulated_norm(x, g, sh, sc):
    ms = jnp.mean(x * x, axis=-1, keepdims=True)
    return (x * lax.rsqrt(ms + EPS)) * g * (1.0 + sc) + sh


def _ada_kernel(c_ref, w_ref, b_ref, o_ref):
    c = c_ref[...]
    cs = c * jax.nn.sigmoid(c)
    o_ref[0] = _dot_f32(cs, w_ref[0]) + b_ref[0]


def _ada(c_pad, ada_w, ada_b):
    depth, d, n = ada_w.shape
    tn = 1024
    return pl.pallas_call(
        _ada_kernel,
        grid=(depth, n // tn),
        in_specs=[
            pl.BlockSpec((c_pad.shape[0], d), lambda l, j: (0, 0)),
            pl.BlockSpec((1, d, tn), lambda l, j: (l, 0, j)),
            pl.BlockSpec((1, 1, tn), lambda l, j: (l, 0, j)),
        ],
        out_specs=pl.BlockSpec((1, c_pad.shape[0], tn), lambda l, j: (l, 0, j)),
        out_shape=jax.ShapeDtypeStruct((depth, c_pad.shape[0], n), F32),
        compiler_params=_cparams("arbitrary", "arbitrary"),
        name="ada_mod",
    )(c_pad, ada_w, ada_b.reshape(depth, 1, n))


def _even_inproj_kernel(x_ref, g_ref, sh_ref, sc_ref, w_ref, pa_ref, q_ref, k_ref, v_ref, *, na, nb):
    h = _modulated_norm(x_ref[...], g_ref[...], sh_ref[0], sc_ref[0]).astype(BF16)
    cw = 512
    for c in range(na // cw):
        pa_ref[:, c * cw:(c + 1) * cw] = _dot(h, w_ref[:, c * cw:(c + 1) * cw])
    q_ref[...] = (_dot(h, w_ref[:, na:na + nb]) * (B_HD ** -0.5)).astype(BF16)
    k_ref[...] = _dot(h, w_ref[:, na + nb:na + 2 * nb]).astype(BF16)
    v_ref[...] = _dot(h, w_ref[:, na + 2 * nb:na + 3 * nb]).astype(BF16)


def _even_inproj(x, g, sh, sc, w_bf16, seq):
    n, d = x.shape
    na = 2 * A_HEADS * A_DK + 2 * A_HEADS * A_DV
    nb = B_HEADS * B_HD
    tm = 512
    tpb = seq // tm
    row = lambda i: (i, 0)
    per_batch = lambda i: (i // tpb, 0, 0)
    return pl.pallas_call(
        functools.partial(_even_inproj_kernel, na=na, nb=nb),
        grid=(n // tm,),
        in_specs=[
            pl.BlockSpec((tm, d), row),
            pl.BlockSpec((1, d), lambda i: (0, 0)),
            pl.BlockSpec((1, 1, d), per_batch),
            pl.BlockSpec((1, 1, d), per_batch),
            pl.BlockSpec(w_bf16.shape, lambda i: (0, 0)),
        ],
        out_specs=[
            pl.BlockSpec((tm, na), row),
            pl.BlockSpec((tm, nb), row),
            pl.BlockSpec((tm, nb), row),
            pl.BlockSpec((tm, nb), row),
        ],
        out_shape=[
            jax.ShapeDtypeStruct((n, na), F32),
            jax.ShapeDtypeStruct((n, nb), BF16),
            jax.ShapeDtypeStruct((n, nb), BF16),
            jax.ShapeDtypeStruct((n, nb), BF16),
        ],
        compiler_params=_cparams("arbitrary"),
        name="even_inproj",
    )(x, g, sh, sc, w_bf16)


HGRN_T = 128
HGRN_GROUP = 4
HGRN_LEVELS = (128, 64, 32, 16, 8)
HGRN_HPG = 4


def _hgrn_constants():
    t = np.arange(HGRN_T)
    j = t[None, :]
    sums, same_block = [], []
    for m in HGRN_LEVELS:
        half = m // 2
        mid = (t // m) * m + half - 1
        sec = (t % m) >= half
        lo = np.where(sec, mid, t)[:, None]
        hi = np.where(sec, t, mid)[:, None]
        sums.append((j > lo) & (j <= hi))
        same_block.append((t[:, None] // m) == (j // m))
    sums.append(j <= t[:, None])
    sums.append(j > t[:, None])
    return -np.concatenate(sums, 0).astype(np.float32), np.stack(same_block).astype(np.float32)


def _roll_in_groups(x, j):
    n, w = x.shape
    return pltpu.roll(x.reshape(n // 8, 8, w), j, 1).reshape(n, w)


def _hgrn_kernel(lbp_ref, ng_ref, nn_ref, blk_ref, q_ref, z_ref, v_ref, ga_ref, o_ref, st_ref,
                 *, layer_j, tblk, hpg):
    t_idx = pl.program_id(2)

    @pl.when(t_idx == 0)
    def _():
        st_ref[...] = jnp.zeros_like(st_ref)

    if layer_j > 0:
        lbp = lbp_ref[...]
        e = jnp.exp(lbp - jnp.max(lbp, axis=0, keepdims=True))
        sm = e / jnp.sum(e, axis=0, keepdims=True)
        lb = sm[1:2]
        for r in range(2, layer_j + 1):
            lb = lb + sm[r:r + 1]
        log_lb = jnp.log(lb)
        log_1m_lb = jnp.log1p(-lb)
    ng = ng_ref[...]

    T = HGRN_T
    nl = len(HGRN_LEVELS)
    row = lax.broadcasted_iota(jnp.int32, (T, A_DK), 0)
    pos = row % HGRN_GROUP
    sec8 = jnp.where(row % 8 >= 4, 1.0, 0.0).astype(F32)

    def step(sb, carry):
        r0 = pl.multiple_of(sb * T, T)
        q2 = q_ref[0, pl.ds(r0, T), :]
        zf = z_ref[0, pl.ds(r0, T), :]
        v2 = v_ref[0, pl.ds(r0, T), :]
        ga2 = ga_ref[0, pl.ds(r0, T), :]
        ls = jnp.minimum(zf, 0.0) - jnp.log(1.0 + jnp.exp(-jnp.abs(zf)))
        if layer_j > 0:
            l2 = log_1m_lb + ls
            mx = jnp.maximum(log_lb, l2)
            a2 = -(mx + jnp.log(1.0 + jnp.exp(-jnp.abs(log_lb - l2))))
            k2 = (1.0 - lb) * jnp.exp(ls - zf)
        else:
            a2 = -ls
            k2 = jnp.exp(ls - zf)
        a_hi = a2.astype(BF16)
        a_lo = (a2 - a_hi.astype(F32)).astype(BF16)
        nn = nn_ref[...]
        dneg = _dot(nn, a_hi) + _dot(nn, a_lo)

        for x in range(hpg):
            hs = slice(x * A_DK, (x + 1) * A_DK)
            q, k, v, a = q2[:, hs], k2[:, hs], v2[:, hs], a2[:, hs]
            vb = v.astype(BF16)

            p = None
            for l, m in enumerate(HGRN_LEVELS):
                g = jnp.exp(dneg[l * T:(l + 1) * T, hs])
                half = m // 2
                if half % 8 == 0:
                    zero = jnp.zeros((half, A_DK), F32)
                    qparts, kparts = [], []
                    for lo in range(0, T, m):
                        mid, hi = lo + half, lo + m
                        qparts += [zero, q[mid:hi] * g[mid:hi]]
                        kparts += [k[lo:mid] * g[lo:mid], zero]
                    qt = jnp.concatenate(qparts, axis=0)
                    kt = jnp.concatenate(kparts, axis=0)
                else:
                    gs = g * sec8
                    qt = q * gs
                    kt = k * (g - gs)
                xl = _dot_nt(qt.astype(BF16), kt.astype(BF16))
                p = xl if p is None else jnp.where(blk_ref[l] > 0.5, xl, p)
            o = _dot(p.astype(BF16), vb)

            o = o + jnp.sum(q * k, axis=-1, keepdims=True) * v
            d = None
            for j in range(1, HGRN_GROUP):
                aj = a if j == 1 else _roll_in_groups(a, j - 1)
                d = aj if d is None else d + aj
                ks = _roll_in_groups(k, j)
                vs = _roll_in_groups(v, j)
                w = jnp.where(pos >= j, jnp.exp(-d), 0.0)
                o = o + jnp.sum(q * ks * w, axis=-1, keepdims=True) * vs

            bfn = dneg[nl * T:(nl + 1) * T, hs]
            to_end = dneg[(nl + 1) * T:(nl + 2) * T, hs]
            st = st_ref[x]
            o = o + _dot_nt((q * jnp.exp(bfn)).astype(BF16), st.astype(BF16))
            kt = (k * jnp.exp(to_end)).astype(BF16)
            st_ref[x] = st * jnp.exp(bfn[T - 1:T]) + _dot_tn(vb, kt)

            y = o * lax.rsqrt(jnp.mean(o * o, axis=-1, keepdims=True) + EPS) * ng[:, hs]
            ga = ga2[:, hs]
            o_ref[0, pl.ds(r0, T), hs] = (y * (ga * jax.nn.sigmoid(ga))).astype(o_ref.dtype)
        return carry

    lax.fori_loop(0, tblk // T, step, 0)


def _hgrn(pa3, hgrn_lb, norm_g, layer_j):
    bsz, seq, _ = pa3.shape
    tblk = 512
    hpg = HGRN_HPG
    ng = A_HEADS // hpg
    gw = hpg * A_DK
    nn, blk = (jnp.asarray(c) for c in _hgrn_constants())
    nn = nn.astype(BF16)
    spec = lambda off: pl.BlockSpec((1, tblk, gw), lambda b, g, t: (b, t, off + g))
    const2 = lambda b, g, t: (0, 0)
    const3 = lambda b, g, t: (0, 0, 0)
    return pl.pallas_call(
        functools.partial(_hgrn_kernel, layer_j=layer_j, tblk=tblk, hpg=hpg),
        grid=(bsz, ng, seq // tblk),
        in_specs=[
            pl.BlockSpec((hgrn_lb.shape[0], gw), lambda b, g, t: (0, g)),
            pl.BlockSpec((1, gw), lambda b, g, t: (0, g)),
            pl.BlockSpec(nn.shape, const2),
            pl.BlockSpec(blk.shape, const3),
            spec(0), spec(ng), spec(2 * ng), spec(3 * ng),
        ],
        out_specs=pl.BlockSpec((1, tblk, gw), lambda b, g, t: (b, t, g)),
        out_shape=jax.ShapeDtypeStruct((bsz, seq, A_HEADS * A_DV), BF16),
        scratch_shapes=[pltpu.VMEM((hpg, A_DV, A_DK), F32)],
        compiler_params=_cparams("arbitrary", "arbitrary", "arbitrary"),
        name="hgrn2",
    )(hgrn_lb, norm_g, nn, blk, pa3, pa3, pa3, pa3)


SB_TQ = 512
SB_TK = 256
SB_HPG = 4


def _sb_kernel(q_ref, k_ref, v_ref, o_ref, acc_ref, out_ref, *, tq, tk, hpg):
    i = pl.program_id(2)
    hd = B_HD
    r = lax.broadcasted_iota(jnp.int32, (tk, tk), 0)
    c = lax.broadcasted_iota(jnp.int32, (tk, tk), 1)
    incl = jnp.where(r >= c, 1.0, 0.0).astype(BF16)
    acc_ref[...] = jnp.zeros_like(acc_ref)
    out_ref[...] = jnp.zeros_like(out_ref)

    def tile(j, masked):
        k0 = pl.multiple_of(j * tk, tk)
        if masked:
            qpos = i * tq + lax.broadcasted_iota(jnp.int32, (tq, tk), 0)
            kpos = j * tk + lax.broadcasted_iota(jnp.int32, (tq, tk), 1)
            strict = kpos < qpos
        for h in range(hpg):
            hs = slice(h * hd, (h + 1) * hd)
            z = _dot_nt(q_ref[:, hs], k_ref[pl.ds(k0, tk), hs])
            sp = jnp.maximum(z, 0.0) + jnp.log(1.0 + jnp.exp2(jnp.abs(z) * (-LOG2E)))
            if masked:
                sp = jnp.where(strict, sp, 0.0)
            rs = _dot(sp.astype(BF16), incl)
            acc = acc_ref[h]
            tot = rs + jnp.concatenate([acc] * (tk // hd), axis=1)
            a = jnp.exp(z - tot)
            if masked:
                a = jnp.where(strict, a, 0.0)
            out_ref[h] += _dot(a.astype(BF16), v_ref[pl.ds(k0, tk), hs])
            acc_ref[h] = jnp.broadcast_to(tot[:, 0:1], (tq, hd))

    ndiag = tq // tk
    for d in range(ndiag):
        tile((i + 1) * ndiag - 1 - d, True)

    def body(s, carry):
        tile(i * ndiag - 1 - s, False)
        return carry

    lax.fori_loop(0, i * ndiag, body, 0)
    for h in range(hpg):
        o_ref[:, h * hd:(h + 1) * hd] = out_ref[h].astype(o_ref.dtype)


def _stick_breaking(q, k, v, bsz, seq):
    n, width = q.shape
    tq, tk, hpg = min(SB_TQ, seq), min(SB_TK, seq), SB_HPG
    gw = hpg * B_HD
    nq = seq // tq
    return pl.pallas_call(
        functools.partial(_sb_kernel, tq=tq, tk=tk, hpg=hpg),
        grid=(bsz, width // gw, nq),
        in_specs=[
            pl.BlockSpec((tq, gw), lambda b, g, i: (b * nq + i, g)),
            pl.BlockSpec((seq, gw), lambda b, g, i: (b, g), pipeline_mode=pl.Buffered(1)),
            pl.BlockSpec((seq, gw), lambda b, g, i: (b, g), pipeline_mode=pl.Buffered(1)),
        ],
        out_specs=pl.BlockSpec((tq, gw), lambda b, g, i: (b * nq + i, g)),
        out_shape=jax.ShapeDtypeStruct((n, width), BF16),
        scratch_shapes=[pltpu.VMEM((hpg, tq, B_HD), F32), pltpu.VMEM((hpg, tq, B_HD), F32)],
        compiler_params=_cparams("arbitrary", "arbitrary", "arbitrary"),
        name="stick_breaking",
    )(q, k, v)


def _outproj_kernel(*refs, nparts):
    x_ref, gate_ref = refs[0], refs[1]
    parts = refs[2:2 + nparts]
    ws = refs[2 + nparts:2 + 2 * nparts]
    o_ref = refs[2 + 2 * nparts]
    y = _dot(parts[0][...], ws[0][...])
    for p, w in zip(parts[1:], ws[1:]):
        y = y + _dot(p[...], w[...])
    o_ref[...] = x_ref[...] + gate_ref[0] * y


def _outproj(x, gate, parts, ws, seq):
    n, d = x.shape
    tm = 1024
    tpb = seq // tm
    row = lambda i: (i, 0)
    nparts = len(parts)
    return pl.pallas_call(
        functools.partial(_outproj_kernel, nparts=nparts),
        grid=(n // tm,),
        in_specs=[pl.BlockSpec((tm, d), row), pl.BlockSpec((1, 1, d), lambda i: (i // tpb, 0, 0))]
        + [pl.BlockSpec((tm, p.shape[1]), row) for p in parts]
        + [pl.BlockSpec(w.shape, lambda i: (0, 0)) for w in ws],
        out_specs=pl.BlockSpec((tm, d), row),
        out_shape=jax.ShapeDtypeStruct((n, d), F32),
        compiler_params=_cparams("arbitrary"),
        name="mixer_outproj",
    )(x, gate, *parts, *ws)


FFN_CHUNK = 256
HALO = 8


def _gelu_tanh(a):
    c = math.sqrt(2.0 / math.pi)
    return 0.5 * a * (1.0 + jnp.tanh(c * (a + 0.044715 * (a * a * a))))


def _ffn_kernel(x_ref, g_ref, sh_ref, sc_ref, gate_ref, win_ref, cw_ref, cb_ref, wout_ref, o_ref,
                halo_ref, abuf_ref, act_ref, *, f, tpb, tm):
    i = pl.program_id(0)

    @pl.when(i % tpb == 0)
    def _():
        halo_ref[...] = jnp.zeros_like(halo_ref)

    x = x_ref[...]
    h = _modulated_norm(x, g_ref[...], sh_ref[0], sc_ref[0]).astype(BF16)
    fc = FFN_CHUNK
    for c in range(f // fc):
        cs = slice(c * fc, (c + 1) * fc)
        a = _dot(h, win_ref[:, cs])
        gg = _dot(h, win_ref[:, f + c * fc:f + (c + 1) * fc])
        abuf_ref[0:HALO, :] = halo_ref[:, cs]
        abuf_ref[HALO:HALO + tm, :] = a
        halo_ref[:, cs] = a[tm - HALO:tm]
        cw = cw_ref[:, cs]
        conv = a * cw[FFN_CONV - 1:FFN_CONV] + cb_ref[:, cs]
        for tap in range(1, FFN_CONV):
            conv = conv + abuf_ref[HALO - tap:HALO - tap + tm, :] * cw[FFN_CONV - 1 - tap:FFN_CONV - tap]
        act_ref[:, cs] = (_gelu_tanh(conv) * gg).astype(BF16)
    y = _dot(act_ref[...], wout_ref[...])
    o_ref[...] = x + gate_ref[0] * y


def _ffn(x, g, sh, sc, gate, win, cw, cb, wout, seq):
    n, d = x.shape
    f = wout.shape[0]
    tm = 512
    tpb = seq // tm
    row = lambda i: (i, 0)
    per_batch = lambda i: (i // tpb, 0, 0)
    const = lambda i: (0, 0)
    return pl.pallas_call(
        functools.partial(_ffn_kernel, f=f, tpb=tpb, tm=tm),
        grid=(n // tm,),
        in_specs=[
            pl.BlockSpec((tm, d), row),
            pl.BlockSpec((1, d), const),
            pl.BlockSpec((1, 1, d), per_batch),
            pl.BlockSpec((1, 1, d), per_batch),
            pl.BlockSpec((1, 1, d), per_batch),
            pl.BlockSpec(win.shape, const, pipeline_mode=pl.Buffered(1)),
            pl.BlockSpec(cw.shape, const),
            pl.BlockSpec(cb.shape, const),
            pl.BlockSpec(wout.shape, const, pipeline_mode=pl.Buffered(1)),
        ],
        out_specs=pl.BlockSpec((tm, d), row),
        out_shape=jax.ShapeDtypeStruct((n, d), F32),
        scratch_shapes=[
            pltpu.VMEM((HALO, f), F32),
            pltpu.VMEM((HALO + tm, FFN_CHUNK), F32),
            pltpu.VMEM((tm, f), BF16),
        ],
        compiler_params=_cparams("arbitrary"),
        name="convglu_ffn",
    )(x, g, sh, sc, gate, win, cw, cb, wout)


def _odd_inproj_kernel(x_ref, g_ref, sh_ref, sc_ref, w_ref, wgt_ref, cw_ref, cb_ref, gb_ref, gbt_ref,
                       q_ref, k_ref, v_ref, og_ref, gc_ref, gr_ref, halo_ref, abuf_ref, *, tpb, tm, nqk, nv):
    i = pl.program_id(0)

    @pl.when(i % tpb == 0)
    def _():
        halo_ref[...] = jnp.zeros_like(halo_ref)

    h = _modulated_norm(x_ref[...], g_ref[...], sh_ref[0], sc_ref[0]).astype(BF16)
    half = nqk // 2
    for c, (dst, scale) in enumerate(((q_ref, 1.0), (k_ref, C_DQK ** -0.5))):
        cs = slice(c * half, (c + 1) * half)
        a = _dot(h, w_ref[:, cs])
        abuf_ref[0:HALO, :] = halo_ref[:, cs]
        abuf_ref[HALO:HALO + tm, :] = a
        halo_ref[:, cs] = a[tm - HALO:tm]
        cw = cw_ref[:, cs]
        conv = a * cw[C_CONV - 1:C_CONV] + cb_ref[:, cs]
        for tap in range(1, C_CONV):
            conv = conv + abuf_ref[HALO - tap:HALO - tap + tm, :] * cw[C_CONV - 1 - tap:C_CONV - tap]
        s = conv * jax.nn.sigmoid(conv)
        dst[...] = (s * scale).astype(BF16)
    for c in range(nv // 512):
        cs = slice(c * 512, (c + 1) * 512)
        v_ref[:, cs] = _dot(h, w_ref[:, nqk + c * 512:nqk + (c + 1) * 512]).astype(BF16)
        og_ref[:, cs] = _dot(h, w_ref[:, nqk + nv + c * 512:nqk + nv + (c + 1) * 512])
    nh = C_HEADS
    gcol = _dot(h, w_ref[:, nqk + 2 * nv:nqk + 2 * nv + 2 * nh]) + gb_ref[...]
    grow = _dot_nt(wgt_ref[...], h) + gbt_ref[...]
    is_f_col = lax.broadcasted_iota(jnp.int32, gcol.shape, 1) >= nh
    is_f_row = lax.broadcasted_iota(jnp.int32, grow.shape, 0) >= nh
    gc_ref[...] = jnp.where(is_f_col, _log_sigmoid(gcol), gcol)
    gr_ref[...] = jnp.where(is_f_row, _log_sigmoid(grow), grow)


def _odd_inproj(x, g, sh, sc, w, wgt, cw, cb, gb, gbt, seq):
    n, d = x.shape
    nqk = 2 * C_HEADS * C_DQK
    nv = C_HEADS * C_DV
    nh2 = 2 * C_HEADS
    tm = 512
    tpb = seq // tm
    row = lambda i: (i, 0)
    per_batch = lambda i: (i // tpb, 0, 0)
    const = lambda i: (0, 0)
    return pl.pallas_call(
        functools.partial(_odd_inproj_kernel, tpb=tpb, tm=tm, nqk=nqk, nv=nv),
        grid=(n // tm,),
        in_specs=[
            pl.BlockSpec((tm, d), row),
            pl.BlockSpec((1, d), const),
            pl.BlockSpec((1, 1, d), per_batch),
            pl.BlockSpec((1, 1, d), per_batch),
            pl.BlockSpec(w.shape, const),
            pl.BlockSpec(wgt.shape, const),
            pl.BlockSpec(cw.shape, const),
            pl.BlockSpec(cb.shape, const),
            pl.BlockSpec(gb.shape, const),
            pl.BlockSpec(gbt.shape, const),
        ],
        out_specs=[
            pl.BlockSpec((tm, nqk // 2), row),
            pl.BlockSpec((tm, nqk // 2), row),
            pl.BlockSpec((tm, nv), row),
            pl.BlockSpec((tm, nv), row),
            pl.BlockSpec((tm, nh2), row),
            pl.BlockSpec((nh2, tm), lambda i: (0, i)),
        ],
        out_shape=[
            jax.ShapeDtypeStruct((n, nqk // 2), BF16),
            jax.ShapeDtypeStruct((n, nqk // 2), BF16),
            jax.ShapeDtypeStruct((n, nv), BF16),
            jax.ShapeDtypeStruct((n, nv), F32),
            jax.ShapeDtypeStruct((n, nh2), F32),
            jax.ShapeDtypeStruct((nh2, n), F32),
        ],
        scratch_shapes=[pltpu.VMEM((HALO, nqk), F32), pltpu.VMEM((HALO + tm, nqk // 2), F32)],
        compiler_params=_cparams("arbitrary"),
        name="odd_inproj",
    )(x, g, sh, sc, w, wgt, cw, cb, gb, gbt)


MLSTM_BLOCK = 256


def _mlstm_kernel(q_ref, k_ref, v_ref, og_ref, gc_ref, gr_ref, ng_ref, o_ref, cm_ref, nv_ref, m_ref):
    L = MLSTM_BLOCK
    dq = C_DQK
    nh = C_HEADS

    @pl.when(pl.program_id(1) == 0)
    def _():
        cm_ref[...] = jnp.zeros_like(cm_ref)
        nv_ref[...] = jnp.zeros_like(nv_ref)
        m_ref[...] = jnp.zeros_like(m_ref)

    r = lax.broadcasted_iota(jnp.int32, (L, L), 0)
    c = lax.broadcasted_iota(jnp.int32, (L, L), 1)
    causal = c <= r
    tril = jnp.where(causal, 1.0, 0.0).astype(F32)
    triu = jnp.where(r <= c, 1.0, 0.0).astype(F32)
    lane = lax.broadcasted_iota(jnp.int32, (1, 2 * dq), 1)
    sub = lax.broadcasted_iota(jnp.int32, (2 * dq, 1), 0)

    gcol = gc_ref[...]
    grow = gr_ref[...]
    bcol_all = _dot_f32(tril, gcol)
    brow_all = _dot_f32(grow, triu)

    for p in range(nh // 2):
        qp = q_ref[:, p * 2 * dq:(p + 1) * 2 * dq]
        kp = k_ref[:, p * 2 * dq:(p + 1) * 2 * dq]
        cm_old = cm_ref[p]
        cm_bf = cm_old.astype(BF16)
        cm_new = jnp.zeros_like(cm_old)
        dec_rows = jnp.zeros((2 * dq, 1), F32)
        for x in range(2):
            h = 2 * p + x
            ig_col = gcol[:, h:h + 1]
            b_col = bcol_all[:, nh + h:nh + h + 1]
            ig_row = grow[h:h + 1, :]
            b_row = brow_all[nh + h:nh + h + 1, :]
            hm = (lane >= x * dq) & (lane < (x + 1) * dq)
            qx = jnp.where(hm, qp, jnp.zeros_like(qp))
            kx = jnp.where(hm, kp, jnp.zeros_like(kp))
            vx = v_ref[:, h * C_DV:(h + 1) * C_DV]
            m_prev = m_ref[h:h + 1, 0:1]
            n_prev = nv_ref[h:h + 1, :]

            log_d = jnp.where(causal, b_col + (ig_row - b_row), -jnp.inf)
            log_inter = b_col + m_prev
            m_t = jnp.maximum(log_inter, jnp.max(log_d, axis=-1, keepdims=True))
            w = _dot_nt(qx, kp) * jnp.exp(log_d - m_t)
            w_inter = jnp.exp(log_inter - m_t)
            num = _dot(w.astype(BF16), vx) + w_inter * _dot(qx, cm_bf)
            qn = jnp.sum(qx.astype(F32) * n_prev, axis=-1, keepdims=True)
            den = jnp.sum(w, axis=-1, keepdims=True) + w_inter * qn
            hc = num / jnp.maximum(jnp.abs(den), jnp.exp(-m_t))

            m_new = m_t[L - 1:L]
            b_last = b_col[L - 1:L]
            w_end = jnp.exp(b_last - b_col + ig_col - m_new)
            decay = jnp.exp(b_last + m_prev - m_new)
            kw = kx.astype(F32) * w_end
            cm_new = cm_new + _dot_tn(kw.astype(BF16), vx)
            dec_rows = dec_rows + jnp.where((sub >= x * dq) & (sub < (x + 1) * dq), decay, 0.0)
            nv_ref[h:h + 1, :] = decay * n_prev + jnp.sum(kw, axis=0, keepdims=True)
            m_ref[h:h + 1, :] = jnp.broadcast_to(m_new, (1, m_ref.shape[1]))

            y = hc * lax.rsqrt(jnp.mean(hc * hc, axis=-1, keepdims=True) + EPS)
            y = y * ng_ref[:, h * C_DV:(h + 1) * C_DV]
            og = og_ref[:, h * C_DV:(h + 1) * C_DV]
            o_ref[:, h * C_DV:(h + 1) * C_DV] = (y * jax.nn.sigmoid(og)).astype(o_ref.dtype)
        cm_ref[p] = dec_rows * cm_old + cm_new


def _mlstm(q, k, v, og, gc, gr, norm_g, bsz, seq):
    n = q.shape[0]
    L = MLSTM_BLOCK
    nt = seq // L
    nh = C_HEADS
    tok = lambda b, t: (b * nt + t, 0)
    return pl.pallas_call(
        _mlstm_kernel,
        grid=(bsz, nt),
        in_specs=[
            pl.BlockSpec((L, nh * C_DQK), tok),
            pl.BlockSpec((L, nh * C_DQK), tok),
            pl.BlockSpec((L, nh * C_DV), tok),
            pl.BlockSpec((L, nh * C_DV), tok),
            pl.BlockSpec((L, 2 * nh), tok),
            pl.BlockSpec((2 * nh, L), lambda b, t: (0, b * nt + t)),
            pl.BlockSpec((1, nh * C_DV), lambda b, t: (0, 0)),
        ],
        out_specs=pl.BlockSpec((L, nh * C_DV), tok),
        out_shape=jax.ShapeDtypeStruct((n, nh * C_DV), BF16),
        scratch_shapes=[
            pltpu.VMEM((nh // 2, 2 * C_DQK, C_DV), F32),
            pltpu.VMEM((nh, 2 * C_DQK), F32),
            pltpu.VMEM((nh, 128), F32),
        ],
        compiler_params=_cparams("arbitrary", "arbitrary"),
        name="mlstm",
    )(q, k, v, og, gc, gr, norm_g)


def _final_norm_kernel(x_ref, g_ref, o_ref):
    x = x_ref[...]
    o_ref[...] = x * lax.rsqrt(jnp.mean(x * x, axis=-1, keepdims=True) + EPS) * g_ref[...]


def _final_norm(x, g):
    n, d = x.shape
    tm = 1024
    return pl.pallas_call(
        _final_norm_kernel,
        grid=(n // tm,),
        in_specs=[pl.BlockSpec((tm, d), lambda i: (i, 0)), pl.BlockSpec((1, d), lambda i: (0, 0))],
        out_specs=pl.BlockSpec((tm, d), lambda i: (i, 0)),
        out_shape=jax.ShapeDtypeStruct((n, d), F32),
        compiler_params=_cparams("arbitrary"),
        name="final_norm",
    )(x, g)


def kernel(x, c, ada_w, ada_b, norm_mix_g, norm_ffn_g, even_w_in, even_w_out, hgrn_lb, hgrn_norm_g,
           odd_w_in, odd_conv_w, odd_conv_b, odd_gate_b, odd_norm_g, odd_w_out, ffn_w_in, ffn_conv_w,
           ffn_conv_b, ffn_w_out, final_g):
    bsz, seq, d = x.shape
    depth = ada_w.shape[0]
    n = bsz * seq
    xf = x.reshape(n, d)

    c_pad = jnp.zeros((8, d), F32).at[:bsz].set(c)
    mod = _ada(c_pad, ada_w, ada_b)[:, :bsz]
    mod = mod.reshape(depth, bsz, 6, 1, d)

    nqk = 2 * C_HEADS * C_DQK
    nv = C_HEADS * C_DV
    for layer in range(depth):
        j = layer // 2
        sh_m, sc_m, g_m, sh_f, sc_f, g_f = (mod[layer, :, i] for i in range(6))
        gmix = norm_mix_g[layer].reshape(1, d)
        if layer % 2 == 0:
            pa, qb, kb, vb = _even_inproj(xf, gmix, sh_m, sc_m, even_w_in[j].astype(BF16), seq)
            oa = _hgrn(pa.reshape(bsz, seq, -1), hgrn_lb, hgrn_norm_g[j].reshape(1, -1), j)
            ob = _stick_breaking(qb, kb, vb, bsz, seq)
            wo = even_w_out[j].astype(BF16)
            na = A_HEADS * A_DV
            xf = _outproj(xf, g_m, [oa.reshape(n, -1), ob], [wo[:na], wo[na:]], seq)
        else:
            w = odd_w_in[j]
            wgt = w[:, nqk + 2 * nv:].T.astype(BF16)
            gb = odd_gate_b[j].reshape(1, -1)
            q, k, v, og, gc, gr = _odd_inproj(
                xf, gmix, sh_m, sc_m, w.astype(BF16), wgt, odd_conv_w[j], odd_conv_b[j].reshape(1, -1),
                gb, gb.reshape(-1, 1), seq)
            o = _mlstm(q, k, v, og, gc, gr, odd_norm_g[j].reshape(1, -1), bsz, seq)
            xf = _outproj(xf, g_m, [o], [odd_w_out[j].astype(BF16)], seq)
        xf = _ffn(xf, norm_ffn_g[layer].reshape(1, d), sh_f, sc_f, g_f, ffn_w_in[layer].astype(BF16),
                  ffn_conv_w[layer], ffn_conv_b[layer].reshape(1, -1), ffn_w_out[layer].astype(BF16), seq)
    return _final_norm(xf, final_g.reshape(1, d)).reshape(bsz, seq, d)
```

```python
import functools
import math

import jax
import jax.numpy as jnp
import numpy as np
from jax import lax
from jax.experimental import pallas as pl
from jax.experimental.pallas import tpu as pltpu

F32 = jnp.float32
BF16 = jnp.bfloat16

EPS = 1e-6
LOG2E = 1.4426950408889634

A_HEADS, A_DK, A_DV = 4, 128, 128
B_HEADS, B_HD = 4, 128
C_HEADS, C_DQK, C_DV = 8, 64, 128
C_CONV = 4
FFN_CONV = 3

VMEM_LIMIT_BYTES = 56 * 1024 * 1024


def _cparams(*sem):
    return pltpu.CompilerParams(dimension_semantics=sem, vmem_limit_bytes=VMEM_LIMIT_BYTES)


def _dot(a, b):
    return jnp.dot(a, b, preferred_element_type=F32)


def _dot_nt(a, b):
    return lax.dot_general(a, b, (((1,), (1,)), ((), ())), preferred_element_type=F32)


def _dot_tn(a, b):
    return lax.dot_general(a, b, (((0,), (0,)), ((), ())), preferred_element_type=F32)


def _dot_f32(a, b):
    return jnp.dot(a, b, preferred_element_type=F32, precision=lax.Precision.HIGHEST)


def _split3(a):
    a1 = a.astype(BF16)
    r1 = a - a1.astype(F32)
    a2 = r1.astype(BF16)
    a3 = (r1 - a2.astype(F32)).astype(BF16)
    return a1, a2, a3


def _log_sigmoid(z):
    return jnp.minimum(z, 0.0) - jnp.log1p(jnp.exp(-jnp.abs(z)))


def _modulated_norm(x, g, sh, sc):
    ms = jnp.mean(x * x, axis=-1, keepdims=True)
    return (x * lax.rsqrt(ms + EPS)) * g * (1.0 + sc) + sh


def _ada_kernel(c_ref, w_ref, b_ref, o_ref):
    c = c_ref[...]
    cs = c * jax.nn.sigmoid(c)
    o_ref[0] = _dot_f32(cs, w_ref[0]) + b_ref[0]


def _ada(c_pad, ada_w, ada_b):
    depth, d, n = ada_w.shape
    tn = 1024
    return pl.pallas_call(
        _ada_kernel,
        grid=(depth, n // tn),
        in_specs=[
            pl.BlockSpec((c_pad.shape[0], d), lambda l, j: (0, 0)),
            pl.BlockSpec((1, d, tn), lambda l, j: (l, 0, j)),
            pl.BlockSpec((1, 1, tn), lambda l, j: (l, 0, j)),
        ],
        out_specs=pl.BlockSpec((1, c_pad.shape[0], tn), lambda l, j: (l, 0, j)),
        out_shape=jax.ShapeDtypeStruct((depth, c_pad.shape[0], n), F32),
        compiler_params=_cparams("arbitrary", "arbitrary"),
        name="ada_mod",
    )(c_pad, ada_w, ada_b.reshape(depth, 1, n))


def _even_inproj_kernel(x_ref, g_ref, sh_ref, sc_ref, w_ref, pa_ref, q_ref, k_ref, v_ref, *, na, nb):
    h = _modulated_norm(x_ref[...], g_ref[...], sh_ref[0], sc_ref[0]).astype(BF16)
    cw = 512
    for c in range(na // cw):
        pa_ref[:, c * cw:(c + 1) * cw] = _dot(h, w_ref[:, c * cw:(c + 1) * cw])
    q_ref[...] = (_dot(h, w_ref[:, na:na + nb]) * (B_HD ** -0.5)).astype(BF16)
    k_ref[...] = _dot(h, w_ref[:, na + nb:na + 2 * nb]).astype(BF16)
    v_ref[...] = _dot(h, w_ref[:, na + 2 * nb:na + 3 * nb]).astype(BF16)


def _even_inproj(x, g, sh, sc, w_bf16, seq):
    n, d = x.shape
    na = 2 * A_HEADS * A_DK + 2 * A_HEADS * A_DV
    nb = B_HEADS * B_HD
    tm = 512
    tpb = seq // tm
    row = lambda i: (i, 0)
    per_batch = lambda i: (i // tpb, 0, 0)
    return pl.pallas_call(
        functools.partial(_even_inproj_kernel, na=na, nb=nb),
        grid=(n // tm,),
        in_specs=[
            pl.BlockSpec((tm, d), row),
            pl.BlockSpec((1, d), lambda i: (0, 0)),
            pl.BlockSpec((1, 1, d), per_batch),
            pl.BlockSpec((1, 1, d), per_batch),
            pl.BlockSpec(w_bf16.shape, lambda i: (0, 0)),
        ],
        out_specs=[
            pl.BlockSpec((tm, na), row),
            pl.BlockSpec((tm, nb), row),
            pl.BlockSpec((tm, nb), row),
            pl.BlockSpec((tm, nb), row),
        ],
        out_shape=[
            jax.ShapeDtypeStruct((n, na), F32),
            jax.ShapeDtypeStruct((n, nb), BF16),
            jax.ShapeDtypeStruct((n, nb), BF16),
            jax.ShapeDtypeStruct((n, nb), BF16),
        ],
        compiler_params=_cparams("arbitrary"),
        name="even_inproj",
    )(x, g, sh, sc, w_bf16)


HGRN_T = 128
HGRN_GROUP = 4
HGRN_LEVELS = (128, 64, 32, 16, 8)
HGRN_HPG = 4


def _hgrn_constants():
    t = np.arange(HGRN_T)
    j = t[None, :]
    sums, same_block = [], []
    for m in HGRN_LEVELS:
        half = m // 2
        mid = (t // m) * m + half - 1
        sec = (t % m) >= half
        lo = np.where(sec, mid, t)[:, None]
        hi = np.where(sec, t, mid)[:, None]
        sums.append((j > lo) & (j <= hi))
        same_block.append((t[:, None] // m) == (j // m))
    sums.append(j <= t[:, None])
    sums.append(j > t[:, None])
    return -np.concatenate(sums, 0).astype(np.float32), np.stack(same_block).astype(np.float32)


def _roll_in_groups(x, j):
    n, w = x.shape
    return pltpu.roll(x.reshape(n // 8, 8, w), j, 1).reshape(n, w)


def _hgrn_kernel(lbp_ref, ng_ref, nn_ref, blk_ref, q_ref, z_ref, v_ref, ga_ref, o_ref, st_ref,
                 *, layer_j, tblk, hpg):
    t_idx = pl.program_id(2)

    @pl.when(t_idx == 0)
    def _():
        st_ref[...] = jnp.zeros_like(st_ref)

    if layer_j > 0:
        lbp = lbp_ref[...]
        e = jnp.exp(lbp - jnp.max(lbp, axis=0, keepdims=True))
        sm = e / jnp.sum(e, axis=0, keepdims=True)
        lb = sm[1:2]
        for r in range(2, layer_j + 1):
            lb = lb + sm[r:r + 1]
        log_lb = jnp.log(lb)
        log_1m_lb = jnp.log1p(-lb)
    ng = ng_ref[...]

    T = HGRN_T
    nl = len(HGRN_LEVELS)
    row = lax.broadcasted_iota(jnp.int32, (T, A_DK), 0)
    pos = row % HGRN_GROUP
    sec8 = jnp.where(row % 8 >= 4, 1.0, 0.0).astype(F32)

    def step(sb, carry):
        r0 = pl.multiple_of(sb * T, T)
        q2 = q_ref[0, pl.ds(r0, T), :]
        zf = z_ref[0, pl.ds(r0, T), :]
        v2 = v_ref[0, pl.ds(r0, T), :]
        ga2 = ga_ref[0, pl.ds(r0, T), :]
        ls = jnp.minimum(zf, 0.0) - jnp.log(1.0 + jnp.exp(-jnp.abs(zf)))
        if layer_j > 0:
            l2 = log_1m_lb + ls
            mx = jnp.maximum(log_lb, l2)
            a2 = -(mx + jnp.log(1.0 + jnp.exp(-jnp.abs(log_lb - l2))))
            k2 = (1.0 - lb) * jnp.exp(ls - zf)
        else:
            a2 = -ls
            k2 = jnp.exp(ls - zf)
        a_hi = a2.astype(BF16)
        a_lo = (a2 - a_hi.astype(F32)).astype(BF16)
        nn = nn_ref[...]
        dneg = _dot(nn, a_hi) + _dot(nn, a_lo)

        for x in range(hpg):
            hs = slice(x * A_DK, (x + 1) * A_DK)
            q, k, v, a = q2[:, hs], k2[:, hs], v2[:, hs], a2[:, hs]
            vb = v.astype(BF16)

            p = None
            for l, m in enumerate(HGRN_LEVELS):
                g = jnp.exp(dneg[l * T:(l + 1) * T, hs])
                half = m // 2
                if half % 8 == 0:
                    zero = jnp.zeros((half, A_DK), F32)
                    qparts, kparts = [], []
                    for lo in range(0, T, m):
                        mid, hi = lo + half, lo + m
                        qparts += [zero, q[mid:hi] * g[mid:hi]]
                        kparts += [k[lo:mid] * g[lo:mid], zero]
                    qt = jnp.concatenate(qparts, axis=0)
                    kt = jnp.concatenate(kparts, axis=0)
                else:
                    gs = g * sec8
                    qt = q * gs
                    kt = k * (g - gs)
                xl = _dot_nt(qt.astype(BF16), kt.astype(BF16))
                p = xl if p is None else jnp.where(blk_ref[l] > 0.5, xl, p)
            o = _dot(p.astype(BF16), vb)

            o = o + jnp.sum(q * k, axis=-1, keepdims=True) * v
            d = None
            for j in range(1, HGRN_GROUP):
                aj = a if j == 1 else _roll_in_groups(a, j - 1)
                d = aj if d is None else d + aj
                ks = _roll_in_groups(k, j)
                vs = _roll_in_groups(v, j)
                w = jnp.where(pos >= j, jnp.exp(-d), 0.0)
                o = o + jnp.sum(q * ks * w, axis=-1, keepdims=True) * vs

            bfn = dneg[nl * T:(nl + 1) * T, hs]
            to_end = dneg[(nl + 1) * T:(nl + 2) * T, hs]
            st = st_ref[x]
            o = o + _dot_nt((q * jnp.exp(bfn)).astype(BF16), st.astype(BF16))
            kt = (k * jnp.exp(to_end)).astype(BF16)
            st_ref[x] = st * jnp.exp(bfn[T - 1:T]) + _dot_tn(vb, kt)

            y = o * lax.rsqrt(jnp.mean(o * o, axis=-1, keepdims=True) + EPS) * ng[:, hs]
            ga = ga2[:, hs]
            o_ref[0, pl.ds(r0, T), hs] = (y * (ga * jax.nn.sigmoid(ga))).astype(o_ref.dtype)
        return carry

    lax.fori_loop(0, tblk // T, step, 0)


def _hgrn(pa3, hgrn_lb, norm_g, layer_j):
    bsz, seq, _ = pa3.shape
    tblk = 512
    hpg = HGRN_HPG
    ng = A_HEADS // hpg
    gw = hpg * A_DK
    nn, blk = (jnp.asarray(c) for c in _hgrn_constants())
    nn = nn.astype(BF16)
    spec = lambda off: pl.BlockSpec((1, tblk, gw), lambda b, g, t: (b, t, off + g))
    const2 = lambda b, g, t: (0, 0)
    const3 = lambda b, g, t: (0, 0, 0)
    return pl.pallas_call(
        functools.partial(_hgrn_kernel, layer_j=layer_j, tblk=tblk, hpg=hpg),
        grid=(bsz, ng, seq // tblk),
        in_specs=[
            pl.BlockSpec((hgrn_lb.shape[0], gw), lambda b, g, t: (0, g)),
            pl.BlockSpec((1, gw), lambda b, g, t: (0, g)),
            pl.BlockSpec(nn.shape, const2),
            pl.BlockSpec(blk.shape, const3),
            spec(0), spec(ng), spec(2 * ng), spec(3 * ng),
        ],
        out_specs=pl.BlockSpec((1, tblk, gw), lambda b, g, t: (b, t, g)),
        out_shape=jax.ShapeDtypeStruct((bsz, seq, A_HEADS * A_DV), BF16),
        scratch_shapes=[pltpu.VMEM((hpg, A_DV, A_DK), F32)],
        compiler_params=_cparams("arbitrary", "arbitrary", "arbitrary"),
        name="hgrn2",
    )(hgrn_lb, norm_g, nn, blk, pa3, pa3, pa3, pa3)


SB_TQ = 512
SB_TK = 256
SB_HPG = 4


def _sb_kernel(q_ref, k_ref, v_ref, o_ref, acc_ref, out_ref, *, tq, tk, hpg):
    i = pl.program_id(2)
    hd = B_HD
    r = lax.broadcasted_iota(jnp.int32, (tk, tk), 0)
    c = lax.broadcasted_iota(jnp.int32, (tk, tk), 1)
    incl = jnp.where(r >= c, 1.0, 0.0).astype(BF16)
    acc_ref[...] = jnp.zeros_like(acc_ref)
    out_ref[...] = jnp.zeros_like(out_ref)

    def tile(j, masked):
        k0 = pl.multiple_of(j * tk, tk)
        if masked:
            qpos = i * tq + lax.broadcasted_iota(jnp.int32, (tq, tk), 0)
            kpos = j * tk + lax.broadcasted_iota(jnp.int32, (tq, tk), 1)
            strict = kpos < qpos
        heads = [slice(h * hd, (h + 1) * hd) for h in range(hpg)]
        zs = [_dot_nt(q_ref[:, hs], k_ref[pl.ds(k0, tk), hs]) for hs in heads]
        sps = []
        for z in zs:
            zb = z.astype(BF16)
            sp = jnp.maximum(zb, 0) + jnp.log(1 + jnp.exp2(jnp.abs(zb) * jnp.asarray(-LOG2E, BF16)))
            if masked:
                sp = jnp.where(strict, sp, jnp.zeros_like(sp))
            sps.append(sp)
        rss = [_dot(sp, incl) for sp in sps]
        for h, hs in enumerate(heads):
            acc = acc_ref[h]
            tot = rss[h] + jnp.concatenate([acc] * (tk // hd), axis=1)
            a = jnp.exp(zs[h] - tot)
            if masked:
                a = jnp.where(strict, a, 0.0)
            out_ref[h] += _dot(a.astype(BF16), v_ref[pl.ds(k0, tk), hs])
            acc_ref[h] = jnp.broadcast_to(tot[:, 0:1], (tq, hd))

    ndiag = tq // tk
    for d in range(ndiag):
        tile((i + 1) * ndiag - 1 - d, True)

    def body(s, carry):
        for d in range(ndiag):
            tile((i - s) * ndiag - 1 - d, False)
        return carry

    lax.fori_loop(0, i, body, 0)
    for h in range(hpg):
        o_ref[:, h * hd:(h + 1) * hd] = out_ref[h].astype(o_ref.dtype)


def _stick_breaking(q, k, v, bsz, seq):
    n, width = q.shape
    tq, tk, hpg = min(SB_TQ, seq), min(SB_TK, seq), SB_HPG
    gw = hpg * B_HD
    nq = seq // tq
    return pl.pallas_call(
        functools.partial(_sb_kernel, tq=tq, tk=tk, hpg=hpg),
        grid=(bsz, width // gw, nq),
        in_specs=[
            pl.BlockSpec((tq, gw), lambda b, g, i: (b * nq + i, g)),
            pl.BlockSpec((seq, gw), lambda b, g, i: (b, g), pipeline_mode=pl.Buffered(1)),
            pl.BlockSpec((seq, gw), lambda b, g, i: (b, g), pipeline_mode=pl.Buffered(1)),
        ],
        out_specs=pl.BlockSpec((tq, gw), lambda b, g, i: (b * nq + i, g)),
        out_shape=jax.ShapeDtypeStruct((n, width), BF16),
        scratch_shapes=[pltpu.VMEM((hpg, tq, B_HD), F32), pltpu.VMEM((hpg, tq, B_HD), F32)],
        compiler_params=_cparams("arbitrary", "arbitrary", "arbitrary"),
        name="stick_breaking",
    )(q, k, v)


def _outproj_kernel(*refs, nparts):
    x_ref, gate_ref = refs[0], refs[1]
    parts = refs[2:2 + nparts]
    ws = refs[2 + nparts:2 + 2 * nparts]
    o_ref = refs[2 + 2 * nparts]
    y = _dot(parts[0][...], ws[0][...])
    for p, w in zip(parts[1:], ws[1:]):
        y = y + _dot(p[...], w[...])
    o_ref[...] = x_ref[...] + gate_ref[0] * y


def _outproj(x, gate, parts, ws, seq):
    n, d = x.shape
    tm = 1024
    tpb = seq // tm
    row = lambda i: (i, 0)
    nparts = len(parts)
    return pl.pallas_call(
        functools.partial(_outproj_kernel, nparts=nparts),
        grid=(n // tm,),
        in_specs=[pl.BlockSpec((tm, d), row), pl.BlockSpec((1, 1, d), lambda i: (i // tpb, 0, 0))]
        + [pl.BlockSpec((tm, p.shape[1]), row) for p in parts]
        + [pl.BlockSpec(w.shape, lambda i: (0, 0)) for w in ws],
        out_specs=pl.BlockSpec((tm, d), row),
        out_shape=jax.ShapeDtypeStruct((n, d), F32),
        compiler_params=_cparams("arbitrary"),
        name="mixer_outproj",
    )(x, gate, *parts, *ws)


FFN_CHUNK = 256
HALO = 8


def _gelu_tanh(a):
    c = math.sqrt(2.0 / math.pi)
    return 0.5 * a * (1.0 + jnp.tanh(c * (a + 0.044715 * (a * a * a))))


def _ffn_kernel(x_ref, g_ref, sh_ref, sc_ref, gate_ref, win_ref, cw_ref, cb_ref, wout_ref, o_ref,
                halo_ref, abuf_ref, act_ref, *, f, tpb, tm):
    i = pl.program_id(0)

    @pl.when(i % tpb == 0)
    def _():
        halo_ref[...] = jnp.zeros_like(halo_ref)

    x = x_ref[...]
    h = _modulated_norm(x, g_ref[...], sh_ref[0], sc_ref[0]).astype(BF16)
    fc = FFN_CHUNK
    for c in range(f // fc):
        cs = slice(c * fc, (c + 1) * fc)
        a = _dot(h, win_ref[:, cs])
        gg = _dot(h, win_ref[:, f + c * fc:f + (c + 1) * fc])
        abuf_ref[0:HALO, :] = halo_ref[:, cs]
        abuf_ref[HALO:HALO + tm, :] = a
        halo_ref[:, cs] = a[tm - HALO:tm]
        cw = cw_ref[:, cs]
        conv = a * cw[FFN_CONV - 1:FFN_CONV] + cb_ref[:, cs]
        for tap in range(1, FFN_CONV):
            conv = conv + abuf_ref[HALO - tap:HALO - tap + tm, :] * cw[FFN_CONV - 1 - tap:FFN_CONV - tap]
        act_ref[:, cs] = (_gelu_tanh(conv) * gg).astype(BF16)
    y = _dot(act_ref[...], wout_ref[...])
    o_ref[...] = x + gate_ref[0] * y


def _ffn(x, g, sh, sc, gate, win, cw, cb, wout, seq):
    n, d = x.shape
    f = wout.shape[0]
    tm = 512
    tpb = seq // tm
    row = lambda i: (i, 0)
    per_batch = lambda i: (i // tpb, 0, 0)
    const = lambda i: (0, 0)
    return pl.pallas_call(
        functools.partial(_ffn_kernel, f=f, tpb=tpb, tm=tm),
        grid=(n // tm,),
        in_specs=[
            pl.BlockSpec((tm, d), row),
            pl.BlockSpec((1, d), const),
            pl.BlockSpec((1, 1, d), per_batch),
            pl.BlockSpec((1, 1, d), per_batch),
            pl.BlockSpec((1, 1, d), per_batch),
            pl.BlockSpec(win.shape, const, pipeline_mode=pl.Buffered(1)),
            pl.BlockSpec(cw.shape, const),
            pl.BlockSpec(cb.shape, const),
            pl.BlockSpec(wout.shape, const, pipeline_mode=pl.Buffered(1)),
        ],
        out_specs=pl.BlockSpec((tm, d), row),
        out_shape=jax.ShapeDtypeStruct((n, d), F32),
        scratch_shapes=[
            pltpu.VMEM((HALO, f), F32),
            pltpu.VMEM((HALO + tm, FFN_CHUNK), F32),
            pltpu.VMEM((tm, f), BF16),
        ],
        compiler_params=_cparams("arbitrary"),
        name="convglu_ffn",
    )(x, g, sh, sc, gate, win, cw, cb, wout)


def _odd_inproj_kernel(x_ref, g_ref, sh_ref, sc_ref, w_ref, wvt_ref, wgt_ref, cw_ref, cb_ref, gb_ref, gbt_ref,
                       q_ref, k_ref, vt_ref, og_ref, gc_ref, gr_ref, halo_ref, abuf_ref, *, tpb, tm, nqk, nv):
    i = pl.program_id(0)

    @pl.when(i % tpb == 0)
    def _():
        halo_ref[...] = jnp.zeros_like(halo_ref)

    h = _modulated_norm(x_ref[...], g_ref[...], sh_ref[0], sc_ref[0]).astype(BF16)
    half = nqk // 2
    for c, (dst, scale) in enumerate(((q_ref, 1.0), (k_ref, C_DQK ** -0.5))):
        cs = slice(c * half, (c + 1) * half)
        a = _dot(h, w_ref[:, cs])
        abuf_ref[0:HALO, :] = halo_ref[:, cs]
        abuf_ref[HALO:HALO + tm, :] = a
        halo_ref[:, cs] = a[tm - HALO:tm]
        cw = cw_ref[:, cs]
        conv = a * cw[C_CONV - 1:C_CONV] + cb_ref[:, cs]
        for tap in range(1, C_CONV):
            conv = conv + abuf_ref[HALO - tap:HALO - tap + tm, :] * cw[C_CONV - 1 - tap:C_CONV - tap]
        s = conv * jax.nn.sigmoid(conv)
        dst[...] = (s * scale).astype(BF16)
    for c in range(nv // 512):
        cs = slice(c * 512, (c + 1) * 512)
        vt_ref[cs, :] = _dot_nt(wvt_ref[cs, :], h).astype(BF16)
        og_ref[:, cs] = _dot(h, w_ref[:, nqk + nv + c * 512:nqk + nv + (c + 1) * 512])
    nh = C_HEADS
    gcol = _dot(h, w_ref[:, nqk + 2 * nv:nqk + 2 * nv + 2 * nh]) + gb_ref[...]
    grow = _dot_nt(wgt_ref[...], h) + gbt_ref[...]
    is_f_col = lax.broadcasted_iota(jnp.int32, gcol.shape, 1) >= nh
    is_f_row = lax.broadcasted_iota(jnp.int32, grow.shape, 0) >= nh
    gc_ref[...] = jnp.where(is_f_col, _log_sigmoid(gcol), gcol)
    gr_ref[...] = jnp.where(is_f_row, _log_sigmoid(grow), grow)


def _odd_inproj(x, g, sh, sc, w, wvt, wgt, cw, cb, gb, gbt, seq):
    n, d = x.shape
    nqk = 2 * C_HEADS * C_DQK
    nv = C_HEADS * C_DV
    nh2 = 2 * C_HEADS
    tm = 512
    tpb = seq // tm
    row = lambda i: (i, 0)
    per_batch = lambda i: (i // tpb, 0, 0)
    const = lambda i: (0, 0)
    return pl.pallas_call(
        functools.partial(_odd_inproj_kernel, tpb=tpb, tm=tm, nqk=nqk, nv=nv),
        grid=(n // tm,),
        in_specs=[
            pl.BlockSpec((tm, d), row),
            pl.BlockSpec((1, d), const),
            pl.BlockSpec((1, 1, d), per_batch),
            pl.BlockSpec((1, 1, d), per_batch),
            pl.BlockSpec(w.shape, const),
            pl.BlockSpec(wvt.shape, const),
            pl.BlockSpec(wgt.shape, const),
            pl.BlockSpec(cw.shape, const),
            pl.BlockSpec(cb.shape, const),
            pl.BlockSpec(gb.shape, const),
            pl.BlockSpec(gbt.shape, const),
        ],
        out_specs=[
            pl.BlockSpec((tm, nqk // 2), row),
            pl.BlockSpec((tm, nqk // 2), row),
            pl.BlockSpec((nv, tm), lambda i: (0, i)),
            pl.BlockSpec((tm, nv), row),
            pl.BlockSpec((tm, nh2), row),
            pl.BlockSpec((nh2, tm), lambda i: (0, i)),
        ],
        out_shape=[
            jax.ShapeDtypeStruct((n, nqk // 2), BF16),
            jax.ShapeDtypeStruct((n, nqk // 2), BF16),
            jax.ShapeDtypeStruct((nv, n), BF16),
            jax.ShapeDtypeStruct((n, nv), F32),
            jax.ShapeDtypeStruct((n, nh2), F32),
            jax.ShapeDtypeStruct((nh2, n), F32),
        ],
        scratch_shapes=[pltpu.VMEM((HALO, nqk), F32), pltpu.VMEM((HALO + tm, nqk // 2), F32)],
        compiler_params=_cparams("arbitrary"),
        name="odd_inproj",
    )(x, g, sh, sc, w, wvt, wgt, cw, cb, gb, gbt)


MLSTM_BLOCK = 256


def _mlstm_kernel(q_ref, k_ref, vt_ref, og_ref, gc_ref, gr_ref, ng_ref, o_ref, ct_ref, nv_ref, m_ref):
    L = MLSTM_BLOCK
    dq = C_DQK
    nh = C_HEADS

    @pl.when(pl.program_id(1) == 0)
    def _():
        ct_ref[...] = jnp.zeros_like(ct_ref)
        nv_ref[...] = jnp.zeros_like(nv_ref)
        m_ref[...] = jnp.zeros_like(m_ref)

    r = lax.broadcasted_iota(jnp.int32, (L, L), 0)
    c = lax.broadcasted_iota(jnp.int32, (L, L), 1)
    causal_t = r <= c
    tril = jnp.where(c <= r, 1.0, 0.0).astype(BF16)
    triu = jnp.where(causal_t, 1.0, 0.0).astype(BF16)
    lane = lax.broadcasted_iota(jnp.int32, (1, 2 * dq), 1)

    gcol = gc_ref[...]
    grow = gr_ref[...]
    bcol_all = sum(_dot(tril, t) for t in _split3(gcol))
    brow_all = sum(_dot(t, triu) for t in _split3(grow))

    for p in range(nh // 2):
        qp = q_ref[:, p * 2 * dq:(p + 1) * 2 * dq]
        kp = k_ref[:, p * 2 * dq:(p + 1) * 2 * dq]
        ct_old = ct_ref[p]
        ct_bf = ct_old.astype(BF16)
        ct_new = jnp.zeros_like(ct_old)
        dec_lanes = jnp.zeros((1, 2 * dq), F32)
        for x in range(2):
            h = 2 * p + x
            ig_col = gcol[:, h:h + 1]
            b_col = bcol_all[:, nh + h:nh + h + 1]
            b_row = brow_all[nh + h:nh + h + 1, :]
            hm = (lane >= x * dq) & (lane < (x + 1) * dq)
            qx = jnp.where(hm, qp, jnp.zeros_like(qp))
            kx = jnp.where(hm, kp, jnp.zeros_like(kp))
            vt = vt_ref[h * C_DV:(h + 1) * C_DV, :]
            m_prev = m_ref[h:h + 1, 0:1]
            n_prev = nv_ref[h:h + 1, :]

            log_d = jnp.where(causal_t, b_row + (ig_col - b_col), -jnp.inf)
            log_inter = b_row + m_prev
            m_t = jnp.maximum(log_inter, jnp.max(log_d, axis=0, keepdims=True))
            w = _dot_nt(kx, qp) * jnp.exp(log_d - m_t)
            w_inter = jnp.exp(log_inter - m_t)
            num = _dot(vt, w.astype(BF16)) + w_inter * _dot_nt(ct_bf, qx)
            n8 = jnp.broadcast_to(n_prev, (8, 2 * dq))
            qn = sum(_dot_nt(t, qx) for t in _split3(n8))[0:1]
            den = jnp.sum(w, axis=0, keepdims=True) + w_inter * qn
            hc = num / jnp.maximum(jnp.abs(den), jnp.exp(-m_t))

            m_new = m_t[:, L - 1:L]
            b_last = b_row[:, L - 1:L]
            w_end = jnp.exp(b_last - b_col + ig_col - m_new)
            decay = jnp.exp(b_last + m_prev - m_new)
            kw = kx.astype(F32) * w_end
            ct_new = ct_new + _dot(vt, kw.astype(BF16))
            dec_lanes = dec_lanes + jnp.where(hm, decay, 0.0)
            nv_ref[h:h + 1, :] = decay * n_prev + jnp.sum(kw, axis=0, keepdims=True)
            m_ref[h:h + 1, :] = jnp.broadcast_to(m_new, (1, m_ref.shape[1]))

            y = (hc * lax.rsqrt(jnp.mean(hc * hc, axis=0, keepdims=True) + EPS)).T
            y = y * ng_ref[:, h * C_DV:(h + 1) * C_DV]
            og = og_ref[:, h * C_DV:(h + 1) * C_DV]
            o_ref[:, h * C_DV:(h + 1) * C_DV] = (y * jax.nn.sigmoid(og)).astype(o_ref.dtype)
        ct_ref[p] = dec_lanes * ct_old + ct_new


def _mlstm(q, k, vt, og, gc, gr, norm_g, bsz, seq):
    n = q.shape[0]
    L = MLSTM_BLOCK
    nt = seq // L
    nh = C_HEADS
    tok = lambda b, t: (b * nt + t, 0)
    chan = lambda b, t: (0, b * nt + t)
    return pl.pallas_call(
        _mlstm_kernel,
        grid=(bsz, nt),
        in_specs=[
            pl.BlockSpec((L, nh * C_DQK), tok),
            pl.BlockSpec((L, nh * C_DQK), tok),
            pl.BlockSpec((nh * C_DV, L), chan),
            pl.BlockSpec((L, nh * C_DV), tok),
            pl.BlockSpec((L, 2 * nh), tok),
            pl.BlockSpec((2 * nh, L), chan),
            pl.BlockSpec((1, nh * C_DV), lambda b, t: (0, 0)),
        ],
        out_specs=pl.BlockSpec((L, nh * C_DV), tok),
        out_shape=jax.ShapeDtypeStruct((n, nh * C_DV), BF16),
        scratch_shapes=[
            pltpu.VMEM((nh // 2, C_DV, 2 * C_DQK), F32),
            pltpu.VMEM((nh, 2 * C_DQK), F32),
            pltpu.VMEM((nh, 128), F32),
        ],
        compiler_params=_cparams("arbitrary", "arbitrary"),
        name="mlstm",
    )(q, k, vt, og, gc, gr, norm_g)


def _final_norm_kernel(x_ref, g_ref, o_ref):
    x = x_ref[...]
    o_ref[...] = x * lax.rsqrt(jnp.mean(x * x, axis=-1, keepdims=True) + EPS) * g_ref[...]


def _final_norm(x, g):
    n, d = x.shape
    tm = 1024
    return pl.pallas_call(
        _final_norm_kernel,
        grid=(n // tm,),
        in_specs=[pl.BlockSpec((tm, d), lambda i: (i, 0)), pl.BlockSpec((1, d), lambda i: (0, 0))],
        out_specs=pl.BlockSpec((tm, d), lambda i: (i, 0)),
        out_shape=jax.ShapeDtypeStruct((n, d), F32),
        compiler_params=_cparams("arbitrary"),
        name="final_norm",
    )(x, g)


def kernel(x, c, ada_w, ada_b, norm_mix_g, norm_ffn_g, even_w_in, even_w_out, hgrn_lb, hgrn_norm_g,
           odd_w_in, odd_conv_w, odd_conv_b, odd_gate_b, odd_norm_g, odd_w_out, ffn_w_in, ffn_conv_w,
           ffn_conv_b, ffn_w_out, final_g):
    bsz, seq, d = x.shape
    depth = ada_w.shape[0]
    n = bsz * seq
    xf = x.reshape(n, d)

    c_pad = jnp.zeros((8, d), F32).at[:bsz].set(c)
    mod = _ada(c_pad, ada_w, ada_b)[:, :bsz]
    mod = mod.reshape(depth, bsz, 6, 1, d)

    nqk = 2 * C_HEADS * C_DQK
    nv = C_HEADS * C_DV
    for layer in range(depth):
        j = layer // 2
        sh_m, sc_m, g_m, sh_f, sc_f, g_f = (mod[layer, :, i] for i in range(6))
        gmix = norm_mix_g[layer].reshape(1, d)
        if layer % 2 == 0:
            pa, qb, kb, vb = _even_inproj(xf, gmix, sh_m, sc_m, even_w_in[j].astype(BF16), seq)
            oa = _hgrn(pa.reshape(bsz, seq, -1), hgrn_lb, hgrn_norm_g[j].reshape(1, -1), j)
            ob = _stick_breaking(qb, kb, vb, bsz, seq)
            wo = even_w_out[j].astype(BF16)
            na = A_HEADS * A_DV
            xf = _outproj(xf, g_m, [oa.reshape(n, -1), ob], [wo[:na], wo[na:]], seq)
        else:
            w = odd_w_in[j]
            wvt = w[:, nqk:nqk + nv].T.astype(BF16)
            wgt = w[:, nqk + 2 * nv:].T.astype(BF16)
            gb = odd_gate_b[j].reshape(1, -1)
            q, k, vt, og, gc, gr = _odd_inproj(
                xf, gmix, sh_m, sc_m, w.astype(BF16), wvt, wgt, odd_conv_w[j], odd_conv_b[j].reshape(1, -1),
                gb, gb.reshape(-1, 1), seq)
            o = _mlstm(q, k, vt, og, gc, gr, odd_norm_g[j].reshape(1, -1), bsz, seq)
            xf = _outproj(xf, g_m, [o], [odd_w_out[j].astype(BF16)], seq)
        xf = _ffn(xf, norm_ffn_g[layer].reshape(1, d), sh_f, sc_f, g_f, ffn_w_in[layer].astype(BF16),
                  ffn_conv_w[layer], ffn_conv_b[layer].reshape(1, -1), ffn_w_out[layer].astype(BF16), seq)
    return _final_norm(xf, final_g.reshape(1, d)).reshape(bsz, seq, d)
```

```python
import functools
import math

import jax
import jax.numpy as jnp
import numpy as np
from jax import lax
from jax.experimental import pallas as pl
from jax.experimental.pallas import tpu as pltpu

F32 = jnp.float32
BF16 = jnp.bfloat16

EPS = 1e-6
LOG2E = 1.4426950408889634

A_HEADS, A_DK, A_DV = 4, 128, 128
B_HEADS, B_HD = 4, 128
C_HEADS, C_DQK, C_DV = 8, 64, 128
C_CONV = 4
FFN_CONV = 3

VMEM_LIMIT_BYTES = 56 * 1024 * 1024


def _cparams(*sem):
    return pltpu.CompilerParams(dimension_semantics=sem, vmem_limit_bytes=VMEM_LIMIT_BYTES)


def _dot(a, b):
    return jnp.dot(a, b, preferred_element_type=F32)


def _dot_nt(a, b):
    return lax.dot_general(a, b, (((1,), (1,)), ((), ())), preferred_element_type=F32)


def _dot_tn(a, b):
    return lax.dot_general(a, b, (((0,), (0,)), ((), ())), preferred_element_type=F32)


def _dot_f32(a, b):
    return jnp.dot(a, b, preferred_element_type=F32, precision=lax.Precision.HIGHEST)


def _split3(a):
    a1 = a.astype(BF16)
    r1 = a - a1.astype(F32)
    a2 = r1.astype(BF16)
    a3 = (r1 - a2.astype(F32)).astype(BF16)
    return a1, a2, a3


def _log_sigmoid(z):
    return jnp.minimum(z, 0.0) - jnp.log1p(jnp.exp(-jnp.abs(z)))


def _modulated_norm(x, g, sh, sc):
    ms = jnp.mean(x * x, axis=-1, keepdims=True)
    return (x * lax.rsqrt(ms + EPS)) * g * (1.0 + sc) + sh


def _ada_kernel(c_ref, w_ref, b_ref, o_ref):
    c = c_ref[...]
    cs = c * jax.nn.sigmoid(c)
    o_ref[0] = _dot_f32(cs, w_ref[0]) + b_ref[0]


def _ada(c_pad, ada_w, ada_b):
    depth, d, n = ada_w.shape
    tn = 1024
    return pl.pallas_call(
        _ada_kernel,
        grid=(depth, n // tn),
        in_specs=[
            pl.BlockSpec((c_pad.shape[0], d), lambda l, j: (0, 0)),
            pl.BlockSpec((1, d, tn), lambda l, j: (l, 0, j)),
            pl.BlockSpec((1, 1, tn), lambda l, j: (l, 0, j)),
        ],
        out_specs=pl.BlockSpec((1, c_pad.shape[0], tn), lambda l, j: (l, 0, j)),
        out_shape=jax.ShapeDtypeStruct((depth, c_pad.shape[0], n), F32),
        compiler_params=_cparams("arbitrary", "arbitrary"),
        name="ada_mod",
    )(c_pad, ada_w, ada_b.reshape(depth, 1, n))


def _even_inproj_kernel(x_ref, g_ref, sh_ref, sc_ref, w_ref, pa_ref, q_ref, k_ref, v_ref, *, na, nb):
    h = _modulated_norm(x_ref[...], g_ref[...], sh_ref[0], sc_ref[0]).astype(BF16)
    cw = 512
    for c in range(na // cw):
        pa_ref[:, c * cw:(c + 1) * cw] = _dot(h, w_ref[:, c * cw:(c + 1) * cw])
    q_ref[...] = (_dot(h, w_ref[:, na:na + nb]) * (B_HD ** -0.5)).astype(BF16)
    k_ref[...] = _dot(h, w_ref[:, na + nb:na + 2 * nb]).astype(BF16)
    v_ref[...] = _dot(h, w_ref[:, na + 2 * nb:na + 3 * nb]).astype(BF16)


def _even_inproj(x, g, sh, sc, w_bf16, seq):
    n, d = x.shape
    na = 2 * A_HEADS * A_DK + 2 * A_HEADS * A_DV
    nb = B_HEADS * B_HD
    tm = 512
    tpb = seq // tm
    row = lambda i: (i, 0)
    per_batch = lambda i: (i // tpb, 0, 0)
    return pl.pallas_call(
        functools.partial(_even_inproj_kernel, na=na, nb=nb),
        grid=(n // tm,),
        in_specs=[
            pl.BlockSpec((tm, d), row),
            pl.BlockSpec((1, d), lambda i: (0, 0)),
            pl.BlockSpec((1, 1, d), per_batch),
            pl.BlockSpec((1, 1, d), per_batch),
            pl.BlockSpec(w_bf16.shape, lambda i: (0, 0)),
        ],
        out_specs=[
            pl.BlockSpec((tm, na), row),
            pl.BlockSpec((tm, nb), row),
            pl.BlockSpec((tm, nb), row),
            pl.BlockSpec((tm, nb), row),
        ],
        out_shape=[
            jax.ShapeDtypeStruct((n, na), F32),
            jax.ShapeDtypeStruct((n, nb), BF16),
            jax.ShapeDtypeStruct((n, nb), BF16),
            jax.ShapeDtypeStruct((n, nb), BF16),
        ],
        compiler_params=_cparams("arbitrary"),
        name="even_inproj",
    )(x, g, sh, sc, w_bf16)


HGRN_T = 128
HGRN_GROUP = 4
HGRN_LEVELS = (128, 64, 32, 16, 8)
HGRN_HPG = 4


def _hgrn_constants():
    t = np.arange(HGRN_T)
    j = t[None, :]
    sums, same_block = [], []
    for m in HGRN_LEVELS:
        half = m // 2
        mid = (t // m) * m + half - 1
        sec = (t % m) >= half
        lo = np.where(sec, mid, t)[:, None]
        hi = np.where(sec, t, mid)[:, None]
        sums.append((j > lo) & (j <= hi))
        same_block.append((t[:, None] // m) == (j // m))
    sums.append(j <= t[:, None])
    sums.append(j > t[:, None])
    return -np.concatenate(sums, 0).astype(np.float32), np.stack(same_block).astype(np.float32)


def _roll_in_groups(x, j):
    n, w = x.shape
    return pltpu.roll(x.reshape(n // 8, 8, w), j, 1).reshape(n, w)


def _hgrn_kernel(lbp_ref, ng_ref, nn_ref, blk_ref, q_ref, z_ref, v_ref, ga_ref, o_ref, st_ref,
                 *, layer_j, tblk, hpg):
    t_idx = pl.program_id(2)

    @pl.when(t_idx == 0)
    def _():
        st_ref[...] = jnp.zeros_like(st_ref)

    if layer_j > 0:
        lbp = lbp_ref[...]
        e = jnp.exp(lbp - jnp.max(lbp, axis=0, keepdims=True))
        sm = e / jnp.sum(e, axis=0, keepdims=True)
        lb = sm[1:2]
        for r in range(2, layer_j + 1):
            lb = lb + sm[r:r + 1]
        log_lb = jnp.log(lb)
        log_1m_lb = jnp.log1p(-lb)
    ng = ng_ref[...]

    T = HGRN_T
    nl = len(HGRN_LEVELS)
    row = lax.broadcasted_iota(jnp.int32, (T, A_DK), 0)
    pos = row % HGRN_GROUP
    sec8 = jnp.where(row % 8 >= 4, 1.0, 0.0).astype(F32)

    def step(sb, carry):
        r0 = pl.multiple_of(sb * T, T)
        q2 = q_ref[0, pl.ds(r0, T), :]
        zf = z_ref[0, pl.ds(r0, T), :]
        v2 = v_ref[0, pl.ds(r0, T), :]
        ga2 = ga_ref[0, pl.ds(r0, T), :]
        ls = jnp.minimum(zf, 0.0) - jnp.log(1.0 + jnp.exp(-jnp.abs(zf)))
        if layer_j > 0:
            l2 = log_1m_lb + ls
            mx = jnp.maximum(log_lb, l2)
            a2 = -(mx + jnp.log(1.0 + jnp.exp(-jnp.abs(log_lb - l2))))
            k2 = (1.0 - lb) * jnp.exp(ls - zf)
        else:
            a2 = -ls
            k2 = jnp.exp(ls - zf)
        a_hi = a2.astype(BF16)
        a_lo = (a2 - a_hi.astype(F32)).astype(BF16)
        nn = nn_ref[...]
        dneg = _dot(nn, a_hi) + _dot(nn, a_lo)

        for x in range(hpg):
            hs = slice(x * A_DK, (x + 1) * A_DK)
            q, k, v, a = q2[:, hs], k2[:, hs], v2[:, hs], a2[:, hs]
            vb = v.astype(BF16)

            p = None
            for l, m in enumerate(HGRN_LEVELS):
                g = jnp.exp(dneg[l * T:(l + 1) * T, hs])
                half = m // 2
                if half % 8 == 0:
                    zero = jnp.zeros((half, A_DK), F32)
                    qparts, kparts = [], []
                    for lo in range(0, T, m):
                        mid, hi = lo + half, lo + m
                        qparts += [zero, q[mid:hi] * g[mid:hi]]
                        kparts += [k[lo:mid] * g[lo:mid], zero]
                    qt = jnp.concatenate(qparts, axis=0)
                    kt = jnp.concatenate(kparts, axis=0)
                else:
                    gs = g * sec8
                    qt = q * gs
                    kt = k * (g - gs)
                xl = _dot_nt(qt.astype(BF16), kt.astype(BF16))
                p = xl if p is None else jnp.where(blk_ref[l] > 0.5, xl, p)
            o = _dot(p.astype(BF16), vb)

            o = o + jnp.sum(q * k, axis=-1, keepdims=True) * v
            d = None
            for j in range(1, HGRN_GROUP):
                aj = a if j == 1 else _roll_in_groups(a, j - 1)
                d = aj if d is None else d + aj
                ks = _roll_in_groups(k, j)
                vs = _roll_in_groups(v, j)
                w = jnp.where(pos >= j, jnp.exp(-d), 0.0)
                o = o + jnp.sum(q * ks * w, axis=-1, keepdims=True) * vs

            bfn = dneg[nl * T:(nl + 1) * T, hs]
            to_end = dneg[(nl + 1) * T:(nl + 2) * T, hs]
            st = st_ref[x]
            o = o + _dot_nt((q * jnp.exp(bfn)).astype(BF16), st.astype(BF16))
            kt = (k * jnp.exp(to_end)).astype(BF16)
            st_ref[x] = st * jnp.exp(bfn[T - 1:T]) + _dot_tn(vb, kt)

            y = o * lax.rsqrt(jnp.mean(o * o, axis=-1, keepdims=True) + EPS) * ng[:, hs]
            ga = ga2[:, hs]
            o_ref[0, pl.ds(r0, T), hs] = (y * (ga * jax.nn.sigmoid(ga))).astype(o_ref.dtype)
        return carry

    lax.fori_loop(0, tblk // T, step, 0, unroll=2)


def _hgrn(pa3, hgrn_lb, norm_g, layer_j):
    bsz, seq, _ = pa3.shape
    tblk = 512
    hpg = HGRN_HPG
    ng = A_HEADS // hpg
    gw = hpg * A_DK
    nn, blk = (jnp.asarray(c) for c in _hgrn_constants())
    nn = nn.astype(BF16)
    spec = lambda off: pl.BlockSpec((1, tblk, gw), lambda b, g, t: (b, t, off + g))
    const2 = lambda b, g, t: (0, 0)
    const3 = lambda b, g, t: (0, 0, 0)
    return pl.pallas_call(
        functools.partial(_hgrn_kernel, layer_j=layer_j, tblk=tblk, hpg=hpg),
        grid=(bsz, ng, seq // tblk),
        in_specs=[
            pl.BlockSpec((hgrn_lb.shape[0], gw), lambda b, g, t: (0, g)),
            pl.BlockSpec((1, gw), lambda b, g, t: (0, g)),
            pl.BlockSpec(nn.shape, const2),
            pl.BlockSpec(blk.shape, const3),
            spec(0), spec(ng), spec(2 * ng), spec(3 * ng),
        ],
        out_specs=pl.BlockSpec((1, tblk, gw), lambda b, g, t: (b, t, g)),
        out_shape=jax.ShapeDtypeStruct((bsz, seq, A_HEADS * A_DV), BF16),
        scratch_shapes=[pltpu.VMEM((hpg, A_DV, A_DK), F32)],
        compiler_params=_cparams("arbitrary", "arbitrary", "arbitrary"),
        name="hgrn2",
    )(hgrn_lb, norm_g, nn, blk, pa3, pa3, pa3, pa3)


SB_TQ = 1024
SB_TK = 256
SB_HPG = 4


def _sb_kernel(q_ref, k_ref, v_ref, o_ref, acc_ref, out_ref, *, tq, tk, hpg):
    i = pl.program_id(2)
    hd = B_HD
    r = lax.broadcasted_iota(jnp.int32, (tk, tk), 0)
    c = lax.broadcasted_iota(jnp.int32, (tk, tk), 1)
    incl = jnp.where(r >= c, 1.0, 0.0).astype(BF16)
    acc_ref[...] = jnp.zeros_like(acc_ref)
    out_ref[...] = jnp.zeros_like(out_ref)

    def tile(j, masked, row0=0):
        k0 = pl.multiple_of(j * tk, tk)
        nr = tq - row0
        if masked:
            qpos = i * tq + row0 + lax.broadcasted_iota(jnp.int32, (nr, tk), 0)
            kpos = j * tk + lax.broadcasted_iota(jnp.int32, (nr, tk), 1)
            strict = kpos < qpos
        heads = [slice(h * hd, (h + 1) * hd) for h in range(hpg)]
        zs = [_dot_nt(q_ref[row0:, hs], k_ref[pl.ds(k0, tk), hs]) for hs in heads]
        sps = []
        for z in zs:
            zb = z.astype(BF16)
            sp = jnp.maximum(zb, 0) + jnp.log(1 + jnp.exp2(jnp.abs(zb) * jnp.asarray(-LOG2E, BF16)))
            if masked:
                sp = jnp.where(strict, sp, jnp.zeros_like(sp))
            sps.append(sp)
        rss = [_dot(sp, incl) for sp in sps]
        for h, hs in enumerate(heads):
            acc = acc_ref[h, row0:, :]
            tot = rss[h] + jnp.concatenate([acc] * (tk // hd), axis=1)
            a = jnp.exp(zs[h] - tot)
            if masked:
                a = jnp.where(strict, a, 0.0)
            out_ref[h, row0:, :] += _dot(a.astype(BF16), v_ref[pl.ds(k0, tk), hs])
            acc_ref[h, row0:, :] = jnp.broadcast_to(tot[:, 0:1], (nr, hd))

    ndiag = tq // tk
    for d in range(ndiag):
        tile((i + 1) * ndiag - 1 - d, True, row0=(ndiag - 1 - d) * tk)

    def body(s, carry):
        for d in range(ndiag):
            tile((i - s) * ndiag - 1 - d, False)
        return carry

    lax.fori_loop(0, i, body, 0)
    for h in range(hpg):
        o_ref[:, h * hd:(h + 1) * hd] = out_ref[h].astype(o_ref.dtype)


def _stick_breaking(q, k, v, bsz, seq):
    n, width = q.shape
    tq, tk, hpg = min(SB_TQ, seq), min(SB_TK, seq), SB_HPG
    gw = hpg * B_HD
    nq = seq // tq
    return pl.pallas_call(
        functools.partial(_sb_kernel, tq=tq, tk=tk, hpg=hpg),
        grid=(bsz, width // gw, nq),
        in_specs=[
            pl.BlockSpec((tq, gw), lambda b, g, i: (b * nq + i, g)),
            pl.BlockSpec((seq, gw), lambda b, g, i: (b, g), pipeline_mode=pl.Buffered(1)),
            pl.BlockSpec((seq, gw), lambda b, g, i: (b, g), pipeline_mode=pl.Buffered(1)),
        ],
        out_specs=pl.BlockSpec((tq, gw), lambda b, g, i: (b * nq + i, g)),
        out_shape=jax.ShapeDtypeStruct((n, width), BF16),
        scratch_shapes=[pltpu.VMEM((hpg, tq, B_HD), F32), pltpu.VMEM((hpg, tq, B_HD), F32)],
        compiler_params=_cparams("arbitrary", "arbitrary", "arbitrary"),
        name="stick_breaking",
    )(q, k, v)


FFN_CHUNK = 256
HALO = 8


def _gelu_tanh(a):
    c = math.sqrt(2.0 / math.pi)
    return 0.5 * a * (1.0 + jnp.tanh(c * (a + 0.044715 * (a * a * a))))


def _ffn_kernel(*refs, nparts, f, tpb, tm, final):
    x_ref, gm_ref = refs[0], refs[1]
    parts = refs[2:2 + nparts]
    ws = refs[2 + nparts:2 + 2 * nparts]
    rest = refs[2 + 2 * nparts:]
    g_ref, sh_ref, sc_ref, gate_ref, win_ref, cw_ref, cb_ref, wout_ref = rest[:8]
    rest = rest[8:]
    if final:
        fg_ref, rest = rest[0], rest[1:]
    o_ref, halo_ref, abuf_ref, act_ref = rest
    i = pl.program_id(0)

    @pl.when(i % tpb == 0)
    def _():
        halo_ref[...] = jnp.zeros_like(halo_ref)

    y = _dot(parts[0][...], ws[0][...])
    for p, w in zip(parts[1:], ws[1:]):
        y = y + _dot(p[...], w[...])
    x = x_ref[...] + gm_ref[0] * y

    h = _modulated_norm(x, g_ref[...], sh_ref[0], sc_ref[0]).astype(BF16)
    fc = FFN_CHUNK
    for c in range(f // fc):
        cs = slice(c * fc, (c + 1) * fc)
        a = _dot(h, win_ref[:, cs])
        gg = _dot(h, win_ref[:, f + c * fc:f + (c + 1) * fc])
        abuf_ref[0:HALO, :] = halo_ref[:, cs]
        abuf_ref[HALO:HALO + tm, :] = a
        halo_ref[:, cs] = a[tm - HALO:tm]
        cw = cw_ref[:, cs]
        conv = a * cw[FFN_CONV - 1:FFN_CONV] + cb_ref[:, cs]
        for tap in range(1, FFN_CONV):
            conv = conv + abuf_ref[HALO - tap:HALO - tap + tm, :] * cw[FFN_CONV - 1 - tap:FFN_CONV - tap]
        act_ref[:, cs] = (_gelu_tanh(conv) * gg).astype(BF16)
    x = x + gate_ref[0] * _dot(act_ref[...], wout_ref[...])
    if final:
        x = x * lax.rsqrt(jnp.mean(x * x, axis=-1, keepdims=True) + EPS) * fg_ref[...]
    o_ref[...] = x


def _mixer_out_ffn(x, gate_m, parts, ws, g, sh, sc, gate, win, cw, cb, wout, seq, final_g=None):
    n, d = x.shape
    f = wout.shape[0]
    tm = 512
    tpb = seq // tm
    row = lambda i: (i, 0)
    per_batch = lambda i: (i // tpb, 0, 0)
    const = lambda i: (0, 0)
    nparts = len(parts)
    final = final_g is not None
    in_specs = (
        [pl.BlockSpec((tm, d), row), pl.BlockSpec((1, 1, d), per_batch)]
        + [pl.BlockSpec((tm, p.shape[1]), row) for p in parts]
        + [pl.BlockSpec(w.shape, const) for w in ws]
        + [
            pl.BlockSpec((1, d), const),
            pl.BlockSpec((1, 1, d), per_batch),
            pl.BlockSpec((1, 1, d), per_batch),
            pl.BlockSpec((1, 1, d), per_batch),
            pl.BlockSpec(win.shape, const, pipeline_mode=pl.Buffered(1)),
            pl.BlockSpec(cw.shape, const),
            pl.BlockSpec(cb.shape, const),
            pl.BlockSpec(wout.shape, const, pipeline_mode=pl.Buffered(1)),
        ]
        + ([pl.BlockSpec((1, d), const)] if final else [])
    )
    args = [x, gate_m, *parts, *ws, g, sh, sc, gate, win, cw, cb, wout] + ([final_g] if final else [])
    return pl.pallas_call(
        functools.partial(_ffn_kernel, nparts=nparts, f=f, tpb=tpb, tm=tm, final=final),
        grid=(n // tm,),
        in_specs=in_specs,
        out_specs=pl.BlockSpec((tm, d), row),
        out_shape=jax.ShapeDtypeStruct((n, d), F32),
        scratch_shapes=[
            pltpu.VMEM((HALO, f), F32),
            pltpu.VMEM((HALO + tm, FFN_CHUNK), F32),
            pltpu.VMEM((tm, f), BF16),
        ],
        compiler_params=_cparams("arbitrary"),
        name="mixer_out_ffn",
    )(*args)


def _odd_inproj_kernel(x_ref, g_ref, sh_ref, sc_ref, w_ref, wvt_ref, wgt_ref, cw_ref, cb_ref, gb_ref, gbt_ref,
                       q_ref, k_ref, vt_ref, og_ref, gc_ref, gr_ref, halo_ref, abuf_ref, *, tpb, tm, nqk, nv):
    i = pl.program_id(0)

    @pl.when(i % tpb == 0)
    def _():
        halo_ref[...] = jnp.zeros_like(halo_ref)

    h = _modulated_norm(x_ref[...], g_ref[...], sh_ref[0], sc_ref[0]).astype(BF16)
    half = nqk // 2
    for c, (dst, scale) in enumerate(((q_ref, 1.0), (k_ref, C_DQK ** -0.5))):
        cs = slice(c * half, (c + 1) * half)
        a = _dot(h, w_ref[:, cs])
        abuf_ref[0:HALO, :] = halo_ref[:, cs]
        abuf_ref[HALO:HALO + tm, :] = a
        halo_ref[:, cs] = a[tm - HALO:tm]
        cw = cw_ref[:, cs]
        conv = a * cw[C_CONV - 1:C_CONV] + cb_ref[:, cs]
        for tap in range(1, C_CONV):
            conv = conv + abuf_ref[HALO - tap:HALO - tap + tm, :] * cw[C_CONV - 1 - tap:C_CONV - tap]
        s = conv * jax.nn.sigmoid(conv)
        dst[...] = (s * scale).astype(BF16)
    for c in range(nv // 512):
        cs = slice(c * 512, (c + 1) * 512)
        vt_ref[cs, :] = _dot_nt(wvt_ref[cs, :], h).astype(BF16)
        og_ref[:, cs] = _dot(h, w_ref[:, nqk + nv + c * 512:nqk + nv + (c + 1) * 512])
    nh = C_HEADS
    gcol = _dot(h, w_ref[:, nqk + 2 * nv:nqk + 2 * nv + 2 * nh]) + gb_ref[...]
    grow = _dot_nt(wgt_ref[...], h) + gbt_ref[...]
    is_f_col = lax.broadcasted_iota(jnp.int32, gcol.shape, 1) >= nh
    is_f_row = lax.broadcasted_iota(jnp.int32, grow.shape, 0) >= nh
    gc_ref[...] = jnp.where(is_f_col, _log_sigmoid(gcol), gcol)
    gr_ref[...] = jnp.where(is_f_row, _log_sigmoid(grow), grow)


def _odd_inproj(x, g, sh, sc, w, wvt, wgt, cw, cb, gb, gbt, seq):
    n, d = x.shape
    nqk = 2 * C_HEADS * C_DQK
    nv = C_HEADS * C_DV
    nh2 = 2 * C_HEADS
    tm = 512
    tpb = seq // tm
    row = lambda i: (i, 0)
    per_batch = lambda i: (i // tpb, 0, 0)
    const = lambda i: (0, 0)
    return pl.pallas_call(
        functools.partial(_odd_inproj_kernel, tpb=tpb, tm=tm, nqk=nqk, nv=nv),
        grid=(n // tm,),
        in_specs=[
            pl.BlockSpec((tm, d), row),
            pl.BlockSpec((1, d), const),
            pl.BlockSpec((1, 1, d), per_batch),
            pl.BlockSpec((1, 1, d), per_batch),
            pl.BlockSpec(w.shape, const),
            pl.BlockSpec(wvt.shape, const),
            pl.BlockSpec(wgt.shape, const),
            pl.BlockSpec(cw.shape, const),
            pl.BlockSpec(cb.shape, const),
            pl.BlockSpec(gb.shape, const),
            pl.BlockSpec(gbt.shape, const),
        ],
        out_specs=[
            pl.BlockSpec((tm, nqk // 2), row),
            pl.BlockSpec((tm, nqk // 2), row),
            pl.BlockSpec((nv, tm), lambda i: (0, i)),
            pl.BlockSpec((tm, nv), row),
            pl.BlockSpec((tm, nh2), row),
            pl.BlockSpec((nh2, tm), lambda i: (0, i)),
        ],
        out_shape=[
            jax.ShapeDtypeStruct((n, nqk // 2), BF16),
            jax.ShapeDtypeStruct((n, nqk // 2), BF16),
            jax.ShapeDtypeStruct((nv, n), BF16),
            jax.ShapeDtypeStruct((n, nv), F32),
            jax.ShapeDtypeStruct((n, nh2), F32),
            jax.ShapeDtypeStruct((nh2, n), F32),
        ],
        scratch_shapes=[pltpu.VMEM((HALO, nqk), F32), pltpu.VMEM((HALO + tm, nqk // 2), F32)],
        compiler_params=_cparams("arbitrary"),
        name="odd_inproj",
    )(x, g, sh, sc, w, wvt, wgt, cw, cb, gb, gbt)


MLSTM_BLOCK = 256


def _mlstm_kernel(q_ref, k_ref, vt_ref, og_ref, gc_ref, gr_ref, ng_ref, o_ref, ct_ref, nv_ref, m_ref):
    L = MLSTM_BLOCK
    dq = C_DQK
    nh = C_HEADS

    @pl.when(pl.program_id(1) == 0)
    def _():
        ct_ref[...] = jnp.zeros_like(ct_ref)
        nv_ref[...] = jnp.zeros_like(nv_ref)
        m_ref[...] = jnp.zeros_like(m_ref)

    r = lax.broadcasted_iota(jnp.int32, (L, L), 0)
    c = lax.broadcasted_iota(jnp.int32, (L, L), 1)
    causal_t = r <= c
    tril = jnp.where(c <= r, 1.0, 0.0).astype(BF16)
    triu = jnp.where(causal_t, 1.0, 0.0).astype(BF16)
    lane = lax.broadcasted_iota(jnp.int32, (1, 2 * dq), 1)

    gcol = gc_ref[...]
    grow = gr_ref[...]
    bcol_all = sum(_dot(tril, t) for t in _split3(gcol))
    brow_all = sum(_dot(t, triu) for t in _split3(grow))

    for p in range(nh // 2):
        qp = q_ref[:, p * 2 * dq:(p + 1) * 2 * dq]
        kp = k_ref[:, p * 2 * dq:(p + 1) * 2 * dq]
        ct_old = ct_ref[p]
        ct_bf = ct_old.astype(BF16)
        ct_new = jnp.zeros_like(ct_old)
        dec_lanes = jnp.zeros((1, 2 * dq), F32)
        for x in range(2):
            h = 2 * p + x
            ig_col = gcol[:, h:h + 1]
            b_col = bcol_all[:, nh + h:nh + h + 1]
            b_row = brow_all[nh + h:nh + h + 1, :]
            hm = (lane >= x * dq) & (lane < (x + 1) * dq)
            qx = jnp.where(hm, qp, jnp.zeros_like(qp))
            kx = jnp.where(hm, kp, jnp.zeros_like(kp))
            vt = vt_ref[h * C_DV:(h + 1) * C_DV, :]
            m_prev = m_ref[h:h + 1, 0:1]
            n_prev = nv_ref[h:h + 1, :]

            log_d = jnp.where(causal_t, b_row + (ig_col - b_col), -jnp.inf)
            log_inter = b_row + m_prev
            m_t = jnp.maximum(log_inter, jnp.max(log_d, axis=0, keepdims=True))
            w = _dot_nt(kx, qp) * jnp.exp(log_d - m_t)
            w_inter = jnp.exp(log_inter - m_t)
            num = _dot(vt, w.astype(BF16)) + w_inter * _dot_nt(ct_bf, qx)
            n8 = jnp.broadcast_to(n_prev, (8, 2 * dq))
            qn = sum(_dot_nt(t, qx) for t in _split3(n8))[0:1]
            den = jnp.sum(w, axis=0, keepdims=True) + w_inter * qn
            hc = num / jnp.maximum(jnp.abs(den), jnp.exp(-m_t))

            m_new = m_t[:, L - 1:L]
            b_last = b_row[:, L - 1:L]
            w_end = jnp.exp(b_last - b_col + ig_col - m_new)
            decay = jnp.exp(b_last + m_prev - m_new)
            kw = kx.astype(F32) * w_end
            ct_new = ct_new + _dot(vt, kw.astype(BF16))
            dec_lanes = dec_lanes + jnp.where(hm, decay, 0.0)
            nv_ref[h:h + 1, :] = decay * n_prev + jnp.sum(kw, axis=0, keepdims=True)
            m_ref[h:h + 1, :] = jnp.broadcast_to(m_new, (1, m_ref.shape[1]))

            y = (hc * lax.rsqrt(jnp.mean(hc * hc, axis=0, keepdims=True) + EPS)).T
            y = y * ng_ref[:, h * C_DV:(h + 1) * C_DV]
            og = og_ref[:, h * C_DV:(h + 1) * C_DV]
            o_ref[:, h * C_DV:(h + 1) * C_DV] = (y * jax.nn.sigmoid(og)).astype(o_ref.dtype)
        ct_ref[p] = dec_lanes * ct_old + ct_new


def _mlstm(q, k, vt, og, gc, gr, norm_g, bsz, seq):
    n = q.shape[0]
    L = MLSTM_BLOCK
    nt = seq // L
    nh = C_HEADS
    tok = lambda b, t: (b * nt + t, 0)
    chan = lambda b, t: (0, b * nt + t)
    return pl.pallas_call(
        _mlstm_kernel,
        grid=(bsz, nt),
        in_specs=[
            pl.BlockSpec((L, nh * C_DQK), tok),
            pl.BlockSpec((L, nh * C_DQK), tok),
            pl.BlockSpec((nh * C_DV, L), chan),
            pl.BlockSpec((L, nh * C_DV), tok),
            pl.BlockSpec((L, 2 * nh), tok),
            pl.BlockSpec((2 * nh, L), chan),
            pl.BlockSpec((1, nh * C_DV), lambda b, t: (0, 0)),
        ],
        out_specs=pl.BlockSpec((L, nh * C_DV), tok),
        out_shape=jax.ShapeDtypeStruct((n, nh * C_DV), BF16),
        scratch_shapes=[
            pltpu.VMEM((nh // 2, C_DV, 2 * C_DQK), F32),
            pltpu.VMEM((nh, 2 * C_DQK), F32),
            pltpu.VMEM((nh, 128), F32),
        ],
        compiler_params=_cparams("arbitrary", "arbitrary"),
        name="mlstm",
    )(q, k, vt, og, gc, gr, norm_g)


def kernel(x, c, ada_w, ada_b, norm_mix_g, norm_ffn_g, even_w_in, even_w_out, hgrn_lb, hgrn_norm_g,
           odd_w_in, odd_conv_w, odd_conv_b, odd_gate_b, odd_norm_g, odd_w_out, ffn_w_in, ffn_conv_w,
           ffn_conv_b, ffn_w_out, final_g):
    bsz, seq, d = x.shape
    depth = ada_w.shape[0]
    n = bsz * seq
    xf = x.reshape(n, d)

    c_pad = jnp.zeros((8, d), F32).at[:bsz].set(c)
    mod = _ada(c_pad, ada_w, ada_b)[:, :bsz]
    mod = mod.reshape(depth, bsz, 6, 1, d)

    nqk = 2 * C_HEADS * C_DQK
    nv = C_HEADS * C_DV
    for layer in range(depth):
        j = layer // 2
        sh_m, sc_m, g_m, sh_f, sc_f, g_f = (mod[layer, :, i] for i in range(6))
        gmix = norm_mix_g[layer].reshape(1, d)
        if layer % 2 == 0:
            pa, qb, kb, vb = _even_inproj(xf, gmix, sh_m, sc_m, even_w_in[j].astype(BF16), seq)
            oa = _hgrn(pa.reshape(bsz, seq, -1), hgrn_lb, hgrn_norm_g[j].reshape(1, -1), j)
            ob = _stick_breaking(qb, kb, vb, bsz, seq)
            wo = even_w_out[j].astype(BF16)
            na = A_HEADS * A_DV
            parts, ws = [oa.reshape(n, -1), ob], [wo[:na], wo[na:]]
        else:
            w = odd_w_in[j]
            wvt = w[:, nqk:nqk + nv].T.astype(BF16)
            wgt = w[:, nqk + 2 * nv:].T.astype(BF16)
            gb = odd_gate_b[j].reshape(1, -1)
            q, k, vt, og, gc, gr = _odd_inproj(
                xf, gmix, sh_m, sc_m, w.astype(BF16), wvt, wgt, odd_conv_w[j], odd_conv_b[j].reshape(1, -1),
                gb, gb.reshape(-1, 1), seq)
            o = _mlstm(q, k, vt, og, gc, gr, odd_norm_g[j].reshape(1, -1), bsz, seq)
            parts, ws = [o], [odd_w_out[j].astype(BF16)]
        xf = _mixer_out_ffn(
            xf, g_m, parts, ws, norm_ffn_g[layer].reshape(1, d), sh_f, sc_f, g_f, ffn_w_in[layer].astype(BF16),
            ffn_conv_w[layer], ffn_conv_b[layer].reshape(1, -1), ffn_w_out[layer].astype(BF16), seq,
            final_g=final_g.reshape(1, d) if layer == depth - 1 else None)
    return xf.reshape(bsz, seq, d)
```
